```python
import jax, jax.numpy as jnp
from jax import lax
import numpy as np

D_MODEL = 1024
BATCH = 8
SEQ = 4096
DEPTH = 1

HEAD_DIM = 128
HEADS_PER_GROUP = 4
DILATED_GROUPS = ((128, 1), (512, 4), (2048, 16))
N_GROUPS = 3
N_ATTN_HEADS = N_GROUPS * HEADS_PER_GROUP
ATTN_WIDTH = N_ATTN_HEADS * HEAD_DIM
ATTN_OUT_WIDTH = HEADS_PER_GROUP * HEAD_DIM
BLOCK = 128
CONV_WIDTH = D_MODEL
CONV_K = 3
D_FF = 4 * D_MODEL
N_MOD = 6
IN_COLS = 3 * ATTN_WIDTH + 3 * CONV_WIDTH + 2 * D_MODEL
EPS = 1e-6
NEG_INF = -1e30

kernel_name = "hybrid_dilated_attn_shortconv_gated_block"


def rmsnorm(x, g):
    xf = x.astype(jnp.float32)
    y = xf * lax.rsqrt(jnp.mean(xf * xf, axis=-1, keepdims=True) + EPS)
    return (y * g.astype(jnp.float32)).astype(x.dtype)


def alibi_slopes(n):
    return 2.0 ** (-8.0 * jnp.arange(1, n + 1, dtype=jnp.float32) / n)


def dilated_window_attention(q, k, v, window, dilation, slopes):
    B, S, H, E = q.shape
    n_win = window // dilation
    span = dilation * BLOCK
    s_pad = -(-S // span) * span
    L = s_pad // dilation
    nb = L // BLOCK

    def to_blocks(t):
        t = jnp.pad(t, ((0, 0), (0, s_pad - S), (0, 0), (0, 0)))
        t = t.reshape(B, L, dilation, H, E).transpose(0, 2, 1, 3, 4)
        return t.reshape(B, dilation, nb, BLOCK, H, E)

    def with_prev(t):
        prev = jnp.concatenate([jnp.zeros_like(t[:, :, :1]), t[:, :, :-1]], axis=2)
        return jnp.concatenate([prev, t], axis=3)

    qb = to_blocks(q)
    kw = with_prev(to_blocks(k))
    vw = with_prev(to_blocks(v))

    scores = jnp.einsum('brnqhe,brnkhe->brnhqk', qb, kw,
                        preferred_element_type=jnp.float32) * (E ** -0.5)
    qi = jnp.arange(BLOCK)[:, None]
    kj = jnp.arange(2 * BLOCK)[None, :]
    delta = BLOCK + qi - kj
    in_window = (delta >= 0) & (delta <= n_win)
    has_key = (jnp.arange(nb)[:, None, None] > 0) | (kj[None] >= BLOCK)
    mask = in_window[None] & has_key
    bias = -slopes[:, None, None] * (delta * dilation).astype(jnp.float32)
    scores = jnp.where(mask[:, None], scores + bias, NEG_INF)

    m = jnp.max(scores, axis=-1, keepdims=True)
    p = jnp.exp(scores - m)
    denom = jnp.sum(p, axis=-1, keepdims=True)
    o = jnp.einsum('brnhqk,brnkhe->brnqhe', p, vw.astype(jnp.float32))
    o = o / jnp.swapaxes(denom, 3, 4)
    lse = (m + jnp.log(denom))[..., 0]

    o = o.reshape(B, dilation, L, H, E).transpose(0, 2, 1, 3, 4).reshape(B, s_pad, H, E)[:, :S]
    lse = lse.transpose(0, 1, 2, 4, 3).reshape(B, dilation, L, H)
    lse = lse.transpose(0, 2, 1, 3).reshape(B, s_pad, H)[:, :S]
    return o, lse


def causal_short_conv(u, w):
    return lax.conv_general_dilated(
        u, w[:, None, :].astype(u.dtype), window_strides=(1,), padding=[(CONV_K - 1, 0)],
        dimension_numbers=('NWC', 'WIO', 'NWC'), feature_group_count=u.shape[-1])


def setup_inputs(seed: int = 0) -> dict:
    key = jax.random.key(seed)
    ks = jax.random.split(key, 16)
    f32 = jnp.float32
    nrm = lambda k, shape, s: jax.random.normal(k, shape, f32) * s
    return {
        "x": jax.random.normal(ks[0], (BATCH, SEQ, D_MODEL), f32),
        "c": jax.random.normal(ks[1], (BATCH, D_MODEL), f32),
        "w_ada": nrm(ks[2], (DEPTH, D_MODEL, N_MOD * D_MODEL), D_MODEL ** -0.5),
        "b_ada": nrm(ks[3], (DEPTH, N_MOD * D_MODEL), 0.01),
        "g_norm_mix": 1.0 + nrm(ks[4], (DEPTH, D_MODEL), 0.02),
        "w_in": nrm(ks[5], (DEPTH, D_MODEL, IN_COLS), D_MODEL ** -0.5),
        "b_gate": nrm(ks[6], (DEPTH, 2 * D_MODEL), 0.01),
        "conv_w": nrm(ks[7], (DEPTH, CONV_K, CONV_WIDTH), CONV_K ** -0.5),
        "w_branch_attn": nrm(ks[8], (DEPTH, ATTN_OUT_WIDTH, D_MODEL), ATTN_OUT_WIDTH ** -0.5),
        "w_branch_conv": nrm(ks[9], (DEPTH, CONV_WIDTH, D_MODEL), CONV_WIDTH ** -0.5),
        "w_out": nrm(ks[10], (DEPTH, D_MODEL, D_MODEL), D_MODEL ** -0.5),
        "g_norm_mlp": 1.0 + nrm(ks[11], (DEPTH, D_MODEL), 0.02),
        "w_mlp_in": nrm(ks[12], (DEPTH, D_MODEL, D_FF), D_MODEL ** -0.5),
        "w_mlp_out": nrm(ks[13], (DEPTH, D_FF, D_MODEL), D_FF ** -0.5),
        "g_norm_final": 1.0 + nrm(ks[14], (D_MODEL,), 0.02),
    }


def reference(x, c, w_ada, b_ada, g_norm_mix, w_in, b_gate, conv_w, w_branch_attn,
              w_branch_conv, w_out, g_norm_mlp, w_mlp_in, w_mlp_out, g_norm_final):
    B, S, D = x.shape
    slopes = alibi_slopes(N_ATTN_HEADS)
    widths = [ATTN_WIDTH] * 3 + [CONV_WIDTH] * 3 + [D_MODEL]
    split_pts = [int(s) for s in np.cumsum(widths)]
    c_act = jax.nn.silu(c)
    for l in range(DEPTH):
        mod = (c_act @ w_ada[l] + b_ada[l])[:, None, :]
        shift1, scale1, gate1, shift2, scale2, gate2 = jnp.split(mod, N_MOD, axis=-1)

        h = rmsnorm(x, g_norm_mix[l]) * (1.0 + scale1) + shift1
        proj = h @ w_in[l]
        q, k, v, cb, cc, cx, g_a, g_b = jnp.split(proj, split_pts, axis=-1)
        q = q.reshape(B, S, N_ATTN_HEADS, HEAD_DIM)
        k = k.reshape(B, S, N_ATTN_HEADS, HEAD_DIM)
        v = v.reshape(B, S, N_ATTN_HEADS, HEAD_DIM)

        outs, lses = [], []
        for gi, (window, dilation) in enumerate(DILATED_GROUPS):
            hs = slice(gi * HEADS_PER_GROUP, (gi + 1) * HEADS_PER_GROUP)
            o_g, lse_g = dilated_window_attention(q[:, :, hs], k[:, :, hs], v[:, :, hs],
                                                  window, dilation, slopes[hs])
            outs.append(o_g)
            lses.append(lse_g)
        w_grp = jax.nn.softmax(jnp.stack(lses), axis=0)
        o_attn = jnp.einsum('gbsh,gbshe->bshe', w_grp, jnp.stack(outs))
        y_attn = o_attn.reshape(B, S, ATTN_OUT_WIDTH).astype(x.dtype) @ w_branch_attn[l]

        u = causal_short_conv(cc * cx, conv_w[l])
        y_conv = (cb * u) @ w_branch_conv[l]

        ba, bb = jnp.split(b_gate[l], 2)
        merged = jax.nn.sigmoid(g_a + ba) * y_attn + jax.nn.sigmoid(g_b + bb) * y_conv
        x = x + gate1 * (merged @ w_out[l])

        h2 = rmsnorm(x, g_norm_mlp[l]) * (1.0 + scale2) + shift2
        x = x + gate2 * (jnp.square(jax.nn.relu(h2 @ w_mlp_in[l])) @ w_mlp_out[l])
    return rmsnorm(x, g_norm_final)
```

```python
import functools

import jax
import jax.numpy as jnp
from jax import lax
from jax.experimental import pallas as pl
from jax.experimental.pallas import tpu as pltpu

D_MODEL = 1024
HEAD_DIM = 128
HEADS_PER_GROUP = 4
DILATED_GROUPS = ((128, 1), (512, 4), (2048, 16))
N_GROUPS = len(DILATED_GROUPS)
N_ATTN_HEADS = N_GROUPS * HEADS_PER_GROUP
ATTN_WIDTH = N_ATTN_HEADS * HEAD_DIM
GROUP_WIDTH = HEADS_PER_GROUP * HEAD_DIM
BLOCK = 128
CONV_K = 3
D_FF = 4 * D_MODEL
N_MOD = 6
QKV_WIDTH = 3 * ATTN_WIDTH
IN_COLS = QKV_WIDTH + 3 * D_MODEL + 2 * D_MODEL
EPS = 1e-6
NEG_INF = -1e30
LANES = 128
SUBLANES = 8
VMEM_LIMIT = 56 * 1024 * 1024

_CB0 = QKV_WIDTH
_CC0 = _CB0 + D_MODEL
_CX0 = _CC0 + D_MODEL
_GA0 = _CX0 + D_MODEL
_GB0 = _GA0 + D_MODEL


def _resident(shape):
    zeros = (0,) * len(shape)
    return pl.BlockSpec(shape, lambda *_: zeros, pipeline_mode=pl.Buffered(1))


def _rms_modulate(xf, g, scale, shift):
    ms = jnp.mean(xf * xf, axis=-1, keepdims=True)
    return (xf * lax.rsqrt(ms + EPS) * g) * (1.0 + scale) + shift


def _ada_kernel(c_ref, w_ref, b_ref, o_ref):
    c = c_ref[...]
    ca = c * jax.nn.sigmoid(c)
    o_ref[...] = jnp.dot(ca, w_ref[...], preferred_element_type=jnp.float32,
                         precision=lax.Precision.HIGHEST) + b_ref[...]


def _ada_call(c, w_ada, b_ada):
    bsz, d = c.shape
    n = w_ada.shape[1]
    return pl.pallas_call(
        _ada_kernel,
        grid=(n // d,),
        in_specs=[pl.BlockSpec((bsz, d), lambda j: (0, 0)),
                  pl.BlockSpec((d, d), lambda j: (0, j)),
                  pl.BlockSpec((1, d), lambda j: (0, j))],
        out_specs=pl.BlockSpec((bsz, d), lambda j: (0, j)),
        out_shape=jax.ShapeDtypeStruct((bsz, n), jnp.float32),
        name="ada_mod",
    )(c, w_ada, b_ada.reshape(1, n))


def _inproj_kernel(x_ref, mod_ref, g_ref, w_ref, bg_ref, cw_ref,
                   qkv_ref, cbu_ref, sa_ref, sb_ref, h_scr, carry_scr):
    tm = x_ref.shape[1]

    @pl.when(pl.program_id(1) == 0)
    def _():
        carry_scr[...] = jnp.zeros_like(carry_scr)

    h = _rms_modulate(x_ref[0], g_ref[...], mod_ref[0, 1:2, :], mod_ref[0, 0:1, :])
    h_scr[...] = h.astype(jnp.bfloat16)

    def proj(c0, width):
        return jnp.dot(h_scr[...], w_ref[:, c0:c0 + width], preferred_element_type=jnp.float32)

    for c0 in range(0, QKV_WIDTH, ATTN_WIDTH):
        qkv_ref[0, :, c0:c0 + ATTN_WIDTH] = proj(c0, ATTN_WIDTH).astype(qkv_ref.dtype)

    p = proj(_CC0, D_MODEL) * proj(_CX0, D_MODEL)
    row = lax.broadcasted_iota(jnp.int32, p.shape, 0)
    prev1 = carry_scr[SUBLANES - 1:SUBLANES, :]
    prev2 = carry_scr[SUBLANES - 2:SUBLANES - 1, :]
    p1 = jnp.where(row == 0, prev1, pltpu.roll(p, 1, 0))
    p2 = jnp.where(row == 0, prev2, jnp.where(row == 1, prev1, pltpu.roll(p, 2, 0)))
    u = cw_ref[0:1, :] * p2 + cw_ref[1:2, :] * p1 + cw_ref[2:3, :] * p
    carry_scr[...] = p[tm - SUBLANES:, :]
    cbu_ref[0] = (proj(_CB0, D_MODEL) * u).astype(cbu_ref.dtype)

    sa_ref[0] = jax.nn.sigmoid(proj(_GA0, D_MODEL) + bg_ref[:, :D_MODEL]).astype(sa_ref.dtype)
    sb_ref[0] = jax.nn.sigmoid(proj(_GB0, D_MODEL) + bg_ref[:, D_MODEL:]).astype(sb_ref.dtype)


def _inproj_call(x, mod, g_mix, w_in, b_gate, conv_w, tm):
    bsz, seq, d = x.shape
    bf = jnp.bfloat16
    row_block = lambda width: pl.BlockSpec((1, tm, width), lambda b, s: (b, s, 0))
    return pl.pallas_call(
        _inproj_kernel,
        grid=(bsz, seq // tm),
        in_specs=[row_block(d),
                  pl.BlockSpec((1, N_MOD, d), lambda b, s: (b, 0, 0)),
                  _resident((1, d)),
                  _resident((d, IN_COLS)),
                  _resident((1, 2 * d)),
                  _resident((CONV_K, d))],
        out_specs=[row_block(QKV_WIDTH), row_block(d), row_block(d), row_block(d)],
        out_shape=[jax.ShapeDtypeStruct((bsz, seq, QKV_WIDTH), bf),
                   jax.ShapeDtypeStruct((bsz, seq, d), bf),
                   jax.ShapeDtypeStruct((bsz, seq, d), bf),
                   jax.ShapeDtypeStruct((bsz, seq, d), bf)],
        scratch_shapes=[pltpu.VMEM((tm, d), bf), pltpu.VMEM((SUBLANES, d), jnp.float32)],
        compiler_params=pltpu.CompilerParams(
            dimension_semantics=("arbitrary", "arbitrary"), vmem_limit_bytes=VMEM_LIMIT),
        name="inproj",
    )(x, mod, g_mix, w_in, b_gate, conv_w)


def _alibi_slope(head):
    return 2.0 ** (-8.0 * (head + 1) / N_ATTN_HEADS)


def _attn_kernel(q_ref, k_ref, v_ref, o_ref, lse_ref, k_scr, v_scr, bias_scr, *, group, dilation):
    rb = q_ref.shape[1]
    step = pl.program_id(2)

    @pl.when(step == 0)
    def _():
        k_scr[0:BLOCK, :] = jnp.zeros((BLOCK, GROUP_WIDTH), k_scr.dtype)
        v_scr[0:BLOCK, :] = jnp.zeros((BLOCK, GROUP_WIDTH), v_scr.dtype)

    @pl.when((pl.program_id(0) == 0) & (pl.program_id(1) == 0) & (step == 0))
    def _():
        qi = lax.broadcasted_iota(jnp.int32, (BLOCK, 2 * BLOCK), 0)
        kj = lax.broadcasted_iota(jnp.int32, (BLOCK, 2 * BLOCK), 1)
        delta = BLOCK + qi - kj
        in_window = (delta >= 0) & (delta <= BLOCK)
        dist = (delta * dilation).astype(jnp.float32)
        for hh in range(HEADS_PER_GROUP):
            slope = _alibi_slope(group * HEADS_PER_GROUP + hh)
            bias_scr[hh] = jnp.where(in_window, -slope * dist, NEG_INF)

    k_scr[BLOCK:, :] = k_ref[0]
    v_scr[BLOCK:, :] = v_ref[0]

    kj = lax.broadcasted_iota(jnp.int32, (BLOCK, 2 * BLOCK), 1)
    scale = HEAD_DIM ** -0.5
    for j in range(rb // BLOCK):
        rows = slice(j * BLOCK, (j + 1) * BLOCK)
        krows = slice(j * BLOCK, (j + 2) * BLOCK)
        lse_cols = []
        for hh in range(HEADS_PER_GROUP):
            cols = slice(hh * HEAD_DIM, (hh + 1) * HEAD_DIM)
            s = lax.dot_general(q_ref[0, rows, cols], k_scr[krows, cols],
                                (((1,), (1,)), ((), ())),
                                preferred_element_type=jnp.float32)
            s = s * scale + bias_scr[hh]
            if j == 0:
                s = jnp.where((step == 0) & (kj < BLOCK), NEG_INF, s)
            m = jnp.max(s, axis=-1, keepdims=True)
            p = jnp.exp(s - m)
            denom = jnp.sum(p, axis=-1, keepdims=True)
            o = jnp.dot(p.astype(v_scr.dtype), v_scr[krows, cols],
                        preferred_element_type=jnp.float32)
            o_ref[0, rows, cols] = o / denom
            lse_cols.append(m + jnp.log(denom))
        lane = lax.broadcasted_iota(jnp.int32, (BLOCK, LANES), 1)
        lse_tile = jnp.zeros((BLOCK, LANES), jnp.float32)
        for hh in range(HEADS_PER_GROUP):
            lse_tile = jnp.where(lane == hh, lse_cols[hh], lse_tile)
        lse_ref[0, rows, :] = lse_tile

    k_scr[0:BLOCK, :] = k_scr[rb:rb + BLOCK, :]
    v_scr[0:BLOCK, :] = v_scr[rb:rb + BLOCK, :]


def _attn_call(qkv, group, dilation, rb):
    bsz, seq, _ = qkv.shape
    sub_len = seq // dilation
    assert sub_len % rb == 0 and rb % BLOCK == 0
    qkv_v = qkv.reshape(bsz, sub_len, dilation * QKV_WIDTH)
    blocks_per_pos = QKV_WIDTH // GROUP_WIDTH

    def in_spec(which):
        return pl.BlockSpec((1, rb, GROUP_WIDTH),
                            lambda b, r, n: (b, n, r * blocks_per_pos + which * N_GROUPS + group))

    o, lse = pl.pallas_call(
        functools.partial(_attn_kernel, group=group, dilation=dilation),
        grid=(bsz, dilation, sub_len // rb),
        in_specs=[in_spec(0), in_spec(1), in_spec(2)],
        out_specs=[pl.BlockSpec((1, rb, GROUP_WIDTH), lambda b, r, n: (b, n, r)),
                   pl.BlockSpec((1, rb, LANES), lambda b, r, n: (b, n, r))],
        out_shape=[jax.ShapeDtypeStruct((bsz, sub_len, dilation * GROUP_WIDTH), jnp.float32),
                   jax.ShapeDtypeStruct((bsz, sub_len, dilation * LANES), jnp.float32)],
        scratch_shapes=[pltpu.VMEM((rb + BLOCK, GROUP_WIDTH), qkv.dtype),
                        pltpu.VMEM((rb + BLOCK, GROUP_WIDTH), qkv.dtype),
                        pltpu.VMEM((HEADS_PER_GROUP, BLOCK, 2 * BLOCK), jnp.float32)],
        compiler_params=pltpu.CompilerParams(
            dimension_semantics=("arbitrary", "arbitrary", "arbitrary")),
        name=f"dilated_attn_g{group}",
    )(qkv_v, qkv_v, qkv_v)
    return o.reshape(bsz, seq, GROUP_WIDTH), lse.reshape(bsz, seq, LANES)


def _tail_kernel(x_ref, mod_ref, o0_ref, o1_ref, o2_ref, l0_ref, l1_ref, l2_ref,
                 cbu_ref, sa_ref, sb_ref, wba_ref, wbc_ref, wo_ref, gm_ref, wmi_ref, wmo_ref,
                 gf_ref, out_ref):
    bf = jnp.bfloat16
    f32 = jnp.float32
    l0, l1, l2 = l0_ref[0], l1_ref[0], l2_ref[0]
    lm = jnp.maximum(jnp.maximum(l0, l1), l2)
    e0, e1, e2 = jnp.exp(l0 - lm), jnp.exp(l1 - lm), jnp.exp(l2 - lm)
    den = e0 + e1 + e2
    w0, w1, w2 = e0 / den, e1 / den, e2 / den
    parts = []
    for hh in range(HEADS_PER_GROUP):
        cols = slice(hh * HEAD_DIM, (hh + 1) * HEAD_DIM)
        parts.append(w0[:, hh:hh + 1] * o0_ref[0, :, cols]
                     + w1[:, hh:hh + 1] * o1_ref[0, :, cols]
                     + w2[:, hh:hh + 1] * o2_ref[0, :, cols])
    o_attn = jnp.concatenate(parts, axis=-1).astype(bf)

    y_attn = jnp.dot(o_attn, wba_ref[...], preferred_element_type=f32)
    y_conv = jnp.dot(cbu_ref[0], wbc_ref[...], preferred_element_type=f32)
    merged = sa_ref[0].astype(f32) * y_attn + sb_ref[0].astype(f32) * y_conv
    x1 = x_ref[0] + mod_ref[0, 2:3, :] * jnp.dot(merged.astype(bf), wo_ref[...],
                                                  preferred_element_type=f32)

    h2 = _rms_modulate(x1, gm_ref[...], mod_ref[0, 4:5, :], mod_ref[0, 3:4, :]).astype(bf)
    a = jnp.dot(h2, wmi_ref[...], preferred_element_type=f32)
    a = jnp.square(jnp.maximum(a, 0.0)).astype(bf)
    x2 = x1 + mod_ref[0, 5:6, :] * jnp.dot(a, wmo_ref[...], preferred_element_type=f32)

    ms = jnp.mean(x2 * x2, axis=-1, keepdims=True)
    out_ref[0] = x2 * lax.rsqrt(ms + EPS) * gf_ref[...]


def _tail_call(x, mod, outs, lses, cbu, sa, sb, w_ba, w_bc, w_out, g_mlp, w_mi, w_mo, g_final, tm):
    bsz, seq, d = x.shape
    row_block = lambda width: pl.BlockSpec((1, tm, width), lambda b, s: (b, s, 0))
    return pl.pallas_call(
        _tail_kernel,
        grid=(bsz, seq // tm),
        in_specs=[row_block(d),
                  pl.BlockSpec((1, N_MOD, d), lambda b, s: (b, 0, 0)),
                  row_block(GROUP_WIDTH), row_block(GROUP_WIDTH), row_block(GROUP_WIDTH),
                  row_block(LANES), row_block(LANES), row_block(LANES),
                  row_block(d), row_block(d), row_block(d),
                  _resident(w_ba.shape), _resident(w_bc.shape), _resident(w_out.shape),
                  _resident((1, d)), _resident(w_mi.shape), _resident(w_mo.shape),
                  _resident((1, d))],
        out_specs=row_block(d),
        out_shape=jax.ShapeDtypeStruct((bsz, seq, d), jnp.float32),
        compiler_params=pltpu.CompilerParams(
            dimension_semantics=("arbitrary", "arbitrary"), vmem_limit_bytes=VMEM_LIMIT),
        name="merge_mlp_tail",
    )(x, mod, *outs, *lses, cbu, sa, sb, w_ba, w_bc, w_out, g_mlp, w_mi, w_mo, g_final)


def kernel(x, c, w_ada, b_ada, g_norm_mix, w_in, b_gate, conv_w, w_branch_attn, w_branch_conv,
           w_out, g_norm_mlp, w_mlp_in, w_mlp_out, g_norm_final):
    bsz, seq, d = x.shape
    depth = w_ada.shape[0]
    bf = jnp.bfloat16
    tm = min(512, seq)
    for l in range(depth):
        mod = _ada_call(c, w_ada[l], b_ada[l]).reshape(bsz, N_MOD, d)
        qkv, cbu, sa, sb = _inproj_call(
            x, mod, g_norm_mix[l].reshape(1, d), w_in[l].astype(bf),
            b_gate[l].reshape(1, 2 * d), conv_w[l], tm)
        outs, lses = [], []
        for gi, (window, dilation) in enumerate(DILATED_GROUPS):
            assert window == dilation * BLOCK
            rb = min(512, seq // dilation)
            o_g, lse_g = _attn_call(qkv, gi, dilation, rb)
            outs.append(o_g)
            lses.append(lse_g)
        last = l == depth - 1
        assert last, "final norm is fused into the last layer's tail"
        x = _tail_call(x, mod, outs, lses, cbu, sa, sb,
                       w_branch_attn[l].astype(bf), w_branch_conv[l].astype(bf),
                       w_out[l].astype(bf), g_norm_mlp[l].reshape(1, d),
                       w_mlp_in[l].astype(bf), w_mlp_out[l].astype(bf),
                       g_norm_final.reshape(1, d), tm)
    return x
```

```python
import functools

import jax
import jax.numpy as jnp
from jax import lax
from jax.experimental import pallas as pl
from jax.experimental.pallas import tpu as pltpu

D_MODEL = 1024
HEAD_DIM = 128
HEADS_PER_GROUP = 4
DILATED_GROUPS = ((128, 1), (512, 4), (2048, 16))
N_GROUPS = len(DILATED_GROUPS)
N_ATTN_HEADS = N_GROUPS * HEADS_PER_GROUP
ATTN_WIDTH = N_ATTN_HEADS * HEAD_DIM
GROUP_WIDTH = HEADS_PER_GROUP * HEAD_DIM
BLOCK = 128
CONV_K = 3
D_FF = 4 * D_MODEL
N_MOD = 6
QKV_WIDTH = 3 * ATTN_WIDTH
IN_COLS = QKV_WIDTH + 3 * D_MODEL + 2 * D_MODEL
EPS = 1e-6
NEG_INF = -1e30
LANES = 128
SUBLANES = 8
VMEM_LIMIT = 56 * 1024 * 1024

_CB0 = QKV_WIDTH
_CC0 = _CB0 + D_MODEL
_CX0 = _CC0 + D_MODEL
_GA0 = _CX0 + D_MODEL
_GB0 = _GA0 + D_MODEL


def _resident(shape):
    zeros = (0,) * len(shape)
    return pl.BlockSpec(shape, lambda *_: zeros, pipeline_mode=pl.Buffered(1))


def _rms_modulate(xf, g, scale, shift):
    ms = jnp.mean(xf * xf, axis=-1, keepdims=True)
    return (xf * lax.rsqrt(ms + EPS) * g) * (1.0 + scale) + shift


def _ada_kernel(c_ref, w_ref, b_ref, o_ref):
    c = c_ref[...]
    ca = c * jax.nn.sigmoid(c)
    o_ref[...] = jnp.dot(ca, w_ref[...], preferred_element_type=jnp.float32,
                         precision=lax.Precision.HIGHEST) + b_ref[...]


def _ada_call(c, w_ada, b_ada):
    bsz, d = c.shape
    n = w_ada.shape[1]
    return pl.pallas_call(
        _ada_kernel,
        grid=(n // d,),
        in_specs=[pl.BlockSpec((bsz, d), lambda j: (0, 0)),
                  pl.BlockSpec((d, d), lambda j: (0, j)),
                  pl.BlockSpec((1, d), lambda j: (0, j))],
        out_specs=pl.BlockSpec((bsz, d), lambda j: (0, j)),
        out_shape=jax.ShapeDtypeStruct((bsz, n), jnp.float32),
        name="ada_mod",
    )(c, w_ada, b_ada.reshape(1, n))


def _inproj_kernel(x_ref, mod_ref, g_ref, w_ref, bg_ref, cw_ref,
                   qkv0_ref, qkv1_ref, qkv2_ref, cbu_ref, sa_ref, sb_ref,
                   h_scr, carry_scr, slab_scr):
    tm = x_ref.shape[1]

    @pl.when(pl.program_id(1) == 0)
    def _():
        carry_scr[...] = jnp.zeros_like(carry_scr)

    h = _rms_modulate(x_ref[0], g_ref[...], mod_ref[0, 1:2, :], mod_ref[0, 0:1, :])
    h_scr[...] = h.astype(jnp.bfloat16)

    def proj(c0, width):
        return jnp.dot(h_scr[...], w_ref[:, c0:c0 + width], preferred_element_type=jnp.float32)

    for gi, (out_ref, (_, d)) in enumerate(zip((qkv0_ref, qkv1_ref, qkv2_ref), DILATED_GROUPS)):
        for which in range(3):
            res = proj(which * ATTN_WIDTH + gi * GROUP_WIDTH, GROUP_WIDTH)
            o0 = which * GROUP_WIDTH
            if d == 1:
                out_ref[0, 0, :, o0:o0 + GROUP_WIDTH] = res.astype(out_ref.dtype)
                continue
            for c in range(GROUP_WIDTH // LANES):
                slab_scr[c] = res[:, c * LANES:(c + 1) * LANES]
            for r in range(d):
                for c in range(GROUP_WIDTH // LANES):
                    out_ref[0, r, :, o0 + c * LANES:o0 + (c + 1) * LANES] = (
                        slab_scr[c, pl.ds(r, tm // d, stride=d), :].astype(out_ref.dtype))

    p = proj(_CC0, D_MODEL) * proj(_CX0, D_MODEL)
    row = lax.broadcasted_iota(jnp.int32, p.shape, 0)
    prev1 = carry_scr[SUBLANES - 1:SUBLANES, :]
    prev2 = carry_scr[SUBLANES - 2:SUBLANES - 1, :]
    p1 = jnp.where(row == 0, prev1, pltpu.roll(p, 1, 0))
    p2 = jnp.where(row == 0, prev2, jnp.where(row == 1, prev1, pltpu.roll(p, 2, 0)))
    u = cw_ref[0:1, :] * p2 + cw_ref[1:2, :] * p1 + cw_ref[2:3, :] * p
    carry_scr[...] = p[tm - SUBLANES:, :]
    cbu_ref[0] = (proj(_CB0, D_MODEL) * u).astype(cbu_ref.dtype)

    sa_ref[0] = jax.nn.sigmoid(proj(_GA0, D_MODEL) + bg_ref[:, :D_MODEL]).astype(sa_ref.dtype)
    sb_ref[0] = jax.nn.sigmoid(proj(_GB0, D_MODEL) + bg_ref[:, D_MODEL:]).astype(sb_ref.dtype)


def _inproj_call(x, mod, g_mix, w_in, b_gate, conv_w, tm):
    bsz, seq, d = x.shape
    bf = jnp.bfloat16
    row_block = lambda width: pl.BlockSpec((1, tm, width), lambda b, s: (b, s, 0))
    qkv_specs, qkv_shapes = [], []
    for _, dil in DILATED_GROUPS:
        qkv_specs.append(pl.BlockSpec((1, dil, tm // dil, 3 * GROUP_WIDTH), lambda b, s: (b, 0, s, 0)))
        qkv_shapes.append(jax.ShapeDtypeStruct((bsz, dil, seq // dil, 3 * GROUP_WIDTH), bf))
    return pl.pallas_call(
        _inproj_kernel,
        grid=(bsz, seq // tm),
        in_specs=[row_block(d),
                  pl.BlockSpec((1, N_MOD, d), lambda b, s: (b, 0, 0)),
                  _resident((1, d)),
                  _resident((d, IN_COLS)),
                  _resident((1, 2 * d)),
                  _resident((CONV_K, d))],
        out_specs=qkv_specs + [row_block(d), row_block(d), row_block(d)],
        out_shape=qkv_shapes + [jax.ShapeDtypeStruct((bsz, seq, d), bf)] * 3,
        scratch_shapes=[pltpu.VMEM((tm, d), bf), pltpu.VMEM((SUBLANES, d), jnp.float32),
                        pltpu.VMEM((GROUP_WIDTH // LANES, tm, LANES), jnp.float32)],
        compiler_params=pltpu.CompilerParams(
            dimension_semantics=("arbitrary", "arbitrary"), vmem_limit_bytes=VMEM_LIMIT),
        name="inproj",
    )(x, mod, g_mix, w_in, b_gate, conv_w)


def _alibi_slope(head):
    return 2.0 ** (-8.0 * (head + 1) / N_ATTN_HEADS)


def _attn_kernel(q_ref, k_ref, v_ref, o_ref, lse_ref, k_scr, v_scr, bias_scr, *, group, dilation):
    rb = q_ref.shape[0]
    step = pl.program_id(2)

    @pl.when(step == 0)
    def _():
        k_scr[0:BLOCK, :] = jnp.zeros((BLOCK, GROUP_WIDTH), k_scr.dtype)
        v_scr[0:BLOCK, :] = jnp.zeros((BLOCK, GROUP_WIDTH), v_scr.dtype)

    @pl.when((pl.program_id(0) == 0) & (pl.program_id(1) == 0) & (step == 0))
    def _():
        qi = lax.broadcasted_iota(jnp.int32, (BLOCK, 2 * BLOCK), 0)
        kj = lax.broadcasted_iota(jnp.int32, (BLOCK, 2 * BLOCK), 1)
        delta = BLOCK + qi - kj
        in_window = (delta >= 0) & (delta <= BLOCK)
        dist = (delta * dilation).astype(jnp.float32)
        for hh in range(HEADS_PER_GROUP):
            slope = _alibi_slope(group * HEADS_PER_GROUP + hh)
            bias_scr[hh] = jnp.where(in_window, -slope * dist, NEG_INF)

    k_scr[BLOCK:, :] = k_ref[...]
    v_scr[BLOCK:, :] = v_ref[...]

    kj = lax.broadcasted_iota(jnp.int32, (BLOCK, 2 * BLOCK), 1)
    scale = HEAD_DIM ** -0.5
    for j in range(rb // BLOCK):
        rows = slice(j * BLOCK, (j + 1) * BLOCK)
        krows = slice(j * BLOCK, (j + 2) * BLOCK)
        lse_cols = []
        for hh in range(HEADS_PER_GROUP):
            cols = slice(hh * HEAD_DIM, (hh + 1) * HEAD_DIM)
            s = lax.dot_general(q_ref[rows, cols], k_scr[krows, cols],
                                (((1,), (1,)), ((), ())),
                                preferred_element_type=jnp.float32)
            s = s * scale + bias_scr[hh]
            if j == 0:
                s = jnp.where((step == 0) & (kj < BLOCK), NEG_INF, s)
            m = jnp.max(s, axis=-1, keepdims=True)
            p = jnp.exp(s - m)
            denom = jnp.sum(p, axis=-1, keepdims=True)
            o = jnp.dot(p.astype(v_scr.dtype), v_scr[krows, cols],
                        preferred_element_type=jnp.float32)
            o_ref[rows, cols] = o / denom
            lse_cols.append(m + jnp.log(denom))
        lane = lax.broadcasted_iota(jnp.int32, (BLOCK, LANES), 1)
        lse_tile = jnp.zeros((BLOCK, LANES), jnp.float32)
        for hh in range(HEADS_PER_GROUP):
            lse_tile = jnp.where(lane == hh, lse_cols[hh], lse_tile)
        lse_ref[rows, :] = lse_tile

    k_scr[0:BLOCK, :] = k_scr[rb:rb + BLOCK, :]
    v_scr[0:BLOCK, :] = v_scr[rb:rb + BLOCK, :]


def _attn_call(qkv, group, rb):
    bsz, dilation, sub_len, _ = qkv.shape
    assert sub_len % rb == 0 and rb % BLOCK == 0

    def spec(width, col):
        return pl.BlockSpec((None, None, rb, width), lambda b, r, n: (b, r, n, col))

    return pl.pallas_call(
        functools.partial(_attn_kernel, group=group, dilation=dilation),
        grid=(bsz, dilation, sub_len // rb),
        in_specs=[spec(GROUP_WIDTH, 0), spec(GROUP_WIDTH, 1), spec(GROUP_WIDTH, 2)],
        out_specs=[spec(GROUP_WIDTH, 0), spec(LANES, 0)],
        out_shape=[jax.ShapeDtypeStruct((bsz, dilation, sub_len, GROUP_WIDTH), jnp.float32),
                   jax.ShapeDtypeStruct((bsz, dilation, sub_len, LANES), jnp.float32)],
        scratch_shapes=[pltpu.VMEM((rb + BLOCK, GROUP_WIDTH), qkv.dtype),
                        pltpu.VMEM((rb + BLOCK, GROUP_WIDTH), qkv.dtype),
                        pltpu.VMEM((HEADS_PER_GROUP, BLOCK, 2 * BLOCK), jnp.float32)],
        compiler_params=pltpu.CompilerParams(
            dimension_semantics=("arbitrary", "arbitrary", "arbitrary")),
        name=f"dilated_attn_g{group}",
    )(qkv, qkv, qkv)


def _tail_kernel(x_ref, mod_ref, o0_ref, o1_ref, o2_ref, l0_ref, l1_ref, l2_ref,
                 cbu_ref, sa_ref, sb_ref, wba_ref, wbc_ref, wo_ref, gm_ref, wmi_ref, wmo_ref,
                 gf_ref, out_ref, nat_scr):
    bf = jnp.bfloat16
    f32 = jnp.float32
    tm = x_ref.shape[1]
    n_slab = HEADS_PER_GROUP + 1

    def natural(gi, o_ref, l_ref):
        d = DILATED_GROUPS[gi][1]
        if d == 1:
            return [o_ref[0, 0, :, hh * HEAD_DIM:(hh + 1) * HEAD_DIM]
                    for hh in range(HEADS_PER_GROUP)] + [l_ref[0, 0]]
        for r in range(d):
            rows = pl.ds(r, tm // d, stride=d)
            for hh in range(HEADS_PER_GROUP):
                nat_scr[gi, hh, rows, :] = o_ref[0, r, :, hh * HEAD_DIM:(hh + 1) * HEAD_DIM]
            nat_scr[gi, HEADS_PER_GROUP, rows, :] = l_ref[0, r]
        return [nat_scr[gi, c] for c in range(n_slab)]

    g0 = natural(0, o0_ref, l0_ref)
    g1 = natural(1, o1_ref, l1_ref)
    g2 = natural(2, o2_ref, l2_ref)

    l0, l1, l2 = g0[-1], g1[-1], g2[-1]
    lm = jnp.maximum(jnp.maximum(l0, l1), l2)
    e0, e1, e2 = jnp.exp(l0 - lm), jnp.exp(l1 - lm), jnp.exp(l2 - lm)
    den = e0 + e1 + e2
    w0, w1, w2 = e0 / den, e1 / den, e2 / den
    parts = []
    for hh in range(HEADS_PER_GROUP):
        parts.append(w0[:, hh:hh + 1] * g0[hh] + w1[:, hh:hh + 1] * g1[hh]
                     + w2[:, hh:hh + 1] * g2[hh])
    o_attn = jnp.concatenate(parts, axis=-1).astype(bf)

    y_attn = jnp.dot(o_attn, wba_ref[...], preferred_element_type=f32)
    y_conv = jnp.dot(cbu_ref[0], wbc_ref[...], preferred_element_type=f32)
    merged = sa_ref[0].astype(f32) * y_attn + sb_ref[0].astype(f32) * y_conv
    x1 = x_ref[0] + mod_ref[0, 2:3, :] * jnp.dot(merged.astype(bf), wo_ref[...],
                                                  preferred_element_type=f32)

    h2 = _rms_modulate(x1, gm_ref[...], mod_ref[0, 4:5, :], mod_ref[0, 3:4, :]).astype(bf)
    a = jnp.dot(h2, wmi_ref[...], preferred_element_type=f32)
    a = jnp.square(jnp.maximum(a, 0.0)).astype(bf)
    x2 = x1 + mod_ref[0, 5:6, :] * jnp.dot(a, wmo_ref[...], preferred_element_type=f32)

    ms = jnp.mean(x2 * x2, axis=-1, keepdims=True)
    out_ref[0] = x2 * lax.rsqrt(ms + EPS) * gf_ref[...]


def _tail_call(x, mod, outs, lses, cbu, sa, sb, w_ba, w_bc, w_out, g_mlp, w_mi, w_mo, g_final, tm):
    bsz, seq, d = x.shape
    row_block = lambda width: pl.BlockSpec((1, tm, width), lambda b, s: (b, s, 0))
    res_block = lambda width: [pl.BlockSpec((1, dil, tm // dil, width), lambda b, s: (b, 0, s, 0))
                               for _, dil in DILATED_GROUPS]
    return pl.pallas_call(
        _tail_kernel,
        grid=(bsz, seq // tm),
        in_specs=[row_block(d),
                  pl.BlockSpec((1, N_MOD, d), lambda b, s: (b, 0, 0)),
                  *res_block(GROUP_WIDTH), *res_block(LANES),
                  row_block(d), row_block(d), row_block(d),
                  _resident(w_ba.shape), _resident(w_bc.shape), _resident(w_out.shape),
                  _resident((1, d)), _resident(w_mi.shape), _resident(w_mo.shape),
                  _resident((1, d))],
        out_specs=row_block(d),
        out_shape=jax.ShapeDtypeStruct((bsz, seq, d), jnp.float32),
        scratch_shapes=[pltpu.VMEM((N_GROUPS, HEADS_PER_GROUP + 1, tm, LANES), jnp.float32)],
        compiler_params=pltpu.CompilerParams(
            dimension_semantics=("arbitrary", "arbitrary"), vmem_limit_bytes=VMEM_LIMIT),
        name="merge_mlp_tail",
    )(x, mod, *outs, *lses, cbu, sa, sb, w_ba, w_bc, w_out, g_mlp, w_mi, w_mo, g_final)


def kernel(x, c, w_ada, b_ada, g_norm_mix, w_in, b_gate, conv_w, w_branch_attn, w_branch_conv,
           w_out, g_norm_mlp, w_mlp_in, w_mlp_out, g_norm_final):
    bsz, seq, d = x.shape
    depth = w_ada.shape[0]
    bf = jnp.bfloat16
    tm = min(512, seq)
    for l in range(depth):
        mod = _ada_call(c, w_ada[l], b_ada[l]).reshape(bsz, N_MOD, d)
        *qkvs, cbu, sa, sb = _inproj_call(
            x, mod, g_norm_mix[l].reshape(1, d), w_in[l].astype(bf),
            b_gate[l].reshape(1, 2 * d), conv_w[l], tm)
        outs, lses = [], []
        for gi, (window, dilation) in enumerate(DILATED_GROUPS):
            assert window == dilation * BLOCK and tm % (dilation * 2 * SUBLANES) == 0
            rb = min(512, seq // dilation)
            o_g, lse_g = _attn_call(qkvs[gi], gi, rb)
            outs.append(o_g)
            lses.append(lse_g)
        last = l == depth - 1
        assert last, "final norm is fused into the last layer's tail"
        x = _tail_call(x, mod, outs, lses, cbu, sa, sb,
                       w_branch_attn[l].astype(bf), w_branch_conv[l].astype(bf),
                       w_out[l].astype(bf), g_norm_mlp[l].reshape(1, d),
                       w_mlp_in[l].astype(bf), w_mlp_out[l].astype(bf),
                       g_norm_final.reshape(1, d), tm)
    return x
```

```python
import math

import jax
import jax.numpy as jnp
from jax import lax
from jax.experimental import pallas as pl
from jax.experimental.pallas import tpu as pltpu

D_MODEL = 1024
HEAD_DIM = 128
HEADS_PER_GROUP = 4
DILATED_GROUPS = ((128, 1), (512, 4), (2048, 16))
N_GROUPS = len(DILATED_GROUPS)
N_ATTN_HEADS = N_GROUPS * HEADS_PER_GROUP
ATTN_WIDTH = N_ATTN_HEADS * HEAD_DIM
GROUP_WIDTH = HEADS_PER_GROUP * HEAD_DIM
BLOCK = 128
SPAN = max(w for w, _ in DILATED_GROUPS)
FAR = N_GROUPS - 1
FAR_PITCH = SPAN // DILATED_GROUPS[FAR][1] + 8
MERGE_ROWS = 16
CONV_K = 3
D_FF = 4 * D_MODEL
N_MOD = 6
QKV_WIDTH = 3 * ATTN_WIDTH
IN_COLS = QKV_WIDTH + 3 * D_MODEL + 2 * D_MODEL
EPS = 1e-6
NEG_INF = -1e30
LOG2E = math.log2(math.e)
LN2 = math.log(2.0)
LANES = 128
SUBLANES = 8
VMEM_LIMIT = 56 * 1024 * 1024

_CB0 = QKV_WIDTH
_CC0 = _CB0 + D_MODEL
_CX0 = _CC0 + D_MODEL
_GA0 = _CX0 + D_MODEL
_GB0 = _GA0 + D_MODEL


def _resident(shape):
    zeros = (0,) * len(shape)
    return pl.BlockSpec(shape, lambda *_: zeros, pipeline_mode=pl.Buffered(1))


def _rms_modulate(xf, g, scale, shift):
    ms = jnp.mean(xf * xf, axis=-1, keepdims=True)
    return (xf * lax.rsqrt(ms + EPS) * g) * (1.0 + scale) + shift


def _ada_kernel(c_ref, w_ref, b_ref, o_ref):
    c = c_ref[...]
    ca = c * jax.nn.sigmoid(c)
    o_ref[...] = jnp.dot(ca, w_ref[...], preferred_element_type=jnp.float32,
                         precision=lax.Precision.HIGHEST) + b_ref[...]


def _ada_call(c, w_ada, b_ada):
    bsz, d = c.shape
    n = w_ada.shape[1]
    return pl.pallas_call(
        _ada_kernel,
        grid=(n // d,),
        in_specs=[pl.BlockSpec((bsz, d), lambda j: (0, 0)),
                  pl.BlockSpec((d, d), lambda j: (0, j)),
                  pl.BlockSpec((1, d), lambda j: (0, j))],
        out_specs=pl.BlockSpec((bsz, d), lambda j: (0, j)),
        out_shape=jax.ShapeDtypeStruct((bsz, n), jnp.float32),
        name="ada_mod",
    )(c, w_ada, b_ada.reshape(1, n))


def _inproj_kernel(x_ref, mod_ref, g_ref, w_ref, bg_ref, cw_ref,
                   qkv0_ref, qkv1_ref, qkv2_ref, cbu_ref, sa_ref, sb_ref,
                   h_scr, carry_scr, slab_scr):
    tm = x_ref.shape[1]
    carry_in = carry_scr.at[(pl.program_id(1) + 1) % 2]
    carry_out = carry_scr.at[pl.program_id(1) % 2]

    @pl.when(pl.program_id(1) == 0)
    def _():
        carry_in[...] = jnp.zeros_like(carry_in)

    h = _rms_modulate(x_ref[0], g_ref[...], mod_ref[0, 1:2, :], mod_ref[0, 0:1, :])
    h_scr[...] = h.astype(jnp.bfloat16)

    def proj(c0, width):
        return jnp.dot(h_scr[...], w_ref[:, c0:c0 + width], preferred_element_type=jnp.float32)

    for gi, (out_ref, (_, d)) in enumerate(zip((qkv0_ref, qkv1_ref, qkv2_ref), DILATED_GROUPS)):
        for which in range(3):
            res = proj(which * ATTN_WIDTH + gi * GROUP_WIDTH, GROUP_WIDTH)
            if which == 0:
                res = res * (HEAD_DIM ** -0.5 * LOG2E)
            o0 = which * GROUP_WIDTH
            if d == 1:
                out_ref[0, 0, :, o0:o0 + GROUP_WIDTH] = res.astype(out_ref.dtype)
                continue
            slab = slab_scr.at[(gi - 1) * 3 + which]
            for c in range(GROUP_WIDTH // LANES):
                slab[c] = res[:, c * LANES:(c + 1) * LANES]
            for r in range(d):
                for c in range(GROUP_WIDTH // LANES):
                    out_ref[0, r, :, o0 + c * LANES:o0 + (c + 1) * LANES] = (
                        slab[c, pl.ds(r, tm // d, stride=d), :].astype(out_ref.dtype))

    p = proj(_CC0, D_MODEL) * proj(_CX0, D_MODEL)
    row = lax.broadcasted_iota(jnp.int32, p.shape, 0)
    prev1 = carry_in[SUBLANES - 1:SUBLANES, :]
    prev2 = carry_in[SUBLANES - 2:SUBLANES - 1, :]
    p1 = jnp.where(row == 0, prev1, pltpu.roll(p, 1, 0))
    p2 = jnp.where(row == 0, prev2, jnp.where(row == 1, prev1, pltpu.roll(p, 2, 0)))
    u = cw_ref[0:1, :] * p2 + cw_ref[1:2, :] * p1 + cw_ref[2:3, :] * p
    carry_out[...] = p[tm - SUBLANES:, :]
    cbu_ref[0] = (proj(_CB0, D_MODEL) * u).astype(cbu_ref.dtype)

    sa_ref[0] = jax.nn.sigmoid(proj(_GA0, D_MODEL) + bg_ref[:, :D_MODEL]).astype(sa_ref.dtype)
    sb_ref[0] = jax.nn.sigmoid(proj(_GB0, D_MODEL) + bg_ref[:, D_MODEL:]).astype(sb_ref.dtype)


def _inproj_call(x, mod, g_mix, w_in, b_gate, conv_w, tm):
    bsz, seq, d = x.shape
    bf = jnp.bfloat16
    row_block = lambda width: pl.BlockSpec((1, tm, width), lambda b, s: (b, s, 0))
    qkv_specs, qkv_shapes = [], []
    for _, dil in DILATED_GROUPS:
        qkv_specs.append(pl.BlockSpec((1, dil, tm // dil, 3 * GROUP_WIDTH), lambda b, s: (b, 0, s, 0)))
        qkv_shapes.append(jax.ShapeDtypeStruct((bsz, dil, seq // dil, 3 * GROUP_WIDTH), bf))
    return pl.pallas_call(
        _inproj_kernel,
        grid=(bsz, seq // tm),
        in_specs=[row_block(d),
                  pl.BlockSpec((1, N_MOD, d), lambda b, s: (b, 0, 0)),
                  _resident((1, d)),
                  _resident((d, IN_COLS)),
                  _resident((1, 2 * d)),
                  _resident((CONV_K, d))],
        out_specs=qkv_specs + [row_block(d), row_block(d), row_block(d)],
        out_shape=qkv_shapes + [jax.ShapeDtypeStruct((bsz, seq, d), bf)] * 3,
        scratch_shapes=[pltpu.VMEM((tm, d), bf), pltpu.VMEM((2, SUBLANES, d), jnp.float32),
                        pltpu.VMEM((3 * (N_GROUPS - 1), GROUP_WIDTH // LANES, tm, LANES), jnp.float32)],
        compiler_params=pltpu.CompilerParams(
            dimension_semantics=("arbitrary", "arbitrary"), vmem_limit_bytes=VMEM_LIMIT),
        name="inproj",
    )(x, mod, g_mix, w_in, b_gate, conv_w)


def _alibi_slope(head):
    return 2.0 ** (-8.0 * (head + 1) / N_ATTN_HEADS)


def _attn_kernel(*refs):
    ins, (out_ref,), scr = refs[:5 * N_GROUPS], refs[5 * N_GROUPS:5 * N_GROUPS + 1], refs[5 * N_GROUPS + 1:]
    q_refs, k_refs, v_refs, kp_refs, vp_refs = (ins[i::5] for i in range(5))
    ks, vs = scr[0:2 * N_GROUPS:2], scr[1:2 * N_GROUPS:2]
    near_scr, far_scr, bias_scr = scr[2 * N_GROUPS:]
    head = pl.program_id(1)
    chunk = pl.program_id(2)
    first_chunk = chunk == 0

    @pl.when((pl.program_id(0) == 0) & (head == 0) & first_chunk)
    def _():
        for v_scr in vs:
            v_scr[:, :, HEAD_DIM:] = jnp.ones(v_scr.shape[:2] + (HEAD_DIM,), v_scr.dtype)
        qi = lax.broadcasted_iota(jnp.int32, (BLOCK, 2 * BLOCK), 0)
        kj = lax.broadcasted_iota(jnp.int32, (BLOCK, 2 * BLOCK), 1)
        delta = BLOCK + qi - kj
        in_window = (delta >= 0) & (delta <= BLOCK)
        for gi, (_, d) in enumerate(DILATED_GROUPS):
            dist = (delta * d).astype(jnp.float32)
            for hh in range(HEADS_PER_GROUP):
                slope = _alibi_slope(gi * HEADS_PER_GROUP + hh) * LOG2E
                bias_scr[0, hh, gi] = jnp.where(in_window, -slope * dist, NEG_INF)
                bias_scr[1, hh, gi] = jnp.where(in_window & (kj >= BLOCK), -slope * dist, NEG_INF)

    for gi in range(N_GROUPS):
        ks[gi][:, 0:BLOCK, :] = kp_refs[gi][...]
        vs[gi][:, 0:BLOCK, 0:HEAD_DIM] = vp_refs[gi][...]
        ks[gi][:, BLOCK:, :] = k_refs[gi][...]
        vs[gi][:, BLOCK:, 0:HEAD_DIM] = v_refs[gi][...]

    first_flag = first_chunk.astype(jnp.int32)

    def body(gi, r, j):
        d = DILATED_GROUPS[gi][1]
        q = q_refs[gi][r, j * BLOCK:(j + 1) * BLOCK, :]
        kk = ks[gi][r, j * BLOCK:(j + 2) * BLOCK, :]
        vv = vs[gi][r, j * BLOCK:(j + 2) * BLOCK, :]
        bias = bias_scr[first_flag if j == 0 else 0, head, gi]
        s = lax.dot_general(q, kk, (((1,), (1,)), ((), ())),
                            preferred_element_type=jnp.float32) + bias
        m = jnp.max(s, axis=-1, keepdims=True)
        p = jnp.exp2(s - m).astype(vv.dtype)
        ov = jnp.dot(p, vv, preferred_element_type=jnp.float32)
        parts = (ov[:, :HEAD_DIM], jnp.broadcast_to(m, (BLOCK, HEAD_DIM)), ov[:, HEAD_DIM:])
        if gi == FAR:
            rows = pl.ds(r * FAR_PITCH + j * BLOCK, BLOCK)
            for kind, val in enumerate(parts):
                far_scr[kind, rows, :] = val
        else:
            start = j * BLOCK * d + r
            rows = pl.ds(start, BLOCK, stride=d) if d > 1 else pl.ds(start, BLOCK)
            for kind, val in enumerate(parts):
                near_scr[gi, kind, rows, :] = val

    d_far = DILATED_GROUPS[FAR][1]

    def far_rows(kind, row0):
        pieces = []
        for t in range(row0, row0 + MERGE_ROWS, SUBLANES):
            pieces.append(far_scr[kind, pl.ds((t % d_far) * FAR_PITCH + t // d_far, SUBLANES,
                                              stride=FAR_PITCH), :])
        return jnp.concatenate(pieces, axis=0)

    def merge(row0):
        rows = pl.ds(row0, MERGE_ROWS)
        o = [near_scr[0, 0, rows, :], near_scr[1, 0, rows, :], far_rows(0, row0)]
        m = [near_scr[0, 1, rows, :], near_scr[1, 1, rows, :], far_rows(1, row0)]
        den = [near_scr[0, 2, rows, :], near_scr[1, 2, rows, :], far_rows(2, row0)]
        top = jnp.maximum(jnp.maximum(m[0], m[1]), m[2])
        a = [jnp.exp2(mg - top) for mg in m]
        num = a[0] * o[0] + a[1] * o[1] + a[2] * o[2]
        tot = a[0] * den[0] + a[1] * den[1] + a[2] * den[2]
        out_ref[rows, :] = (num / tot).astype(out_ref.dtype)

    sub = [SPAN // d // BLOCK for _, d in DILATED_GROUPS]
    for r in range(DILATED_GROUPS[2][1]):
        for j in range(sub[2]):
            body(2, r, j)
    stretch = SPAN // sub[1]
    for j1 in range(sub[1]):
        for r in range(DILATED_GROUPS[1][1]):
            body(1, r, j1)
        for j0 in range(j1 * sub[0] // sub[1], (j1 + 1) * sub[0] // sub[1]):
            body(0, 0, j0)
        for row0 in range(j1 * stretch, (j1 + 1) * stretch, MERGE_ROWS):
            merge(row0)


def _attn_call(qkvs):
    bsz = qkvs[0].shape[0]
    seq = qkvs[0].shape[1] * qkvs[0].shape[2]
    assert seq % SPAN == 0
    bf = qkvs[0].dtype
    in_specs, scratch = [], []
    for _, d in DILATED_GROUPS:
        nb = SPAN // d // BLOCK
        for which in range(3):
            in_specs.append(pl.BlockSpec(
                (None, d, SPAN // d, HEAD_DIM),
                lambda b, h, n, which=which: (b, 0, n, which * HEADS_PER_GROUP + h)))
        for which in (1, 2):
            in_specs.append(pl.BlockSpec(
                (None, d, BLOCK, HEAD_DIM),
                lambda b, h, n, which=which, nb=nb: (b, 0, jnp.maximum(n * nb - 1, 0),
                                                     which * HEADS_PER_GROUP + h)))
        scratch += [pltpu.VMEM((d, SPAN // d + BLOCK, HEAD_DIM), bf),
                    pltpu.VMEM((d, SPAN // d + BLOCK, 2 * HEAD_DIM), bf)]
    scratch += [pltpu.VMEM((FAR, 3, SPAN, HEAD_DIM), jnp.float32),
                pltpu.VMEM((3, DILATED_GROUPS[FAR][1] * FAR_PITCH, HEAD_DIM), jnp.float32),
                pltpu.VMEM((2, HEADS_PER_GROUP, N_GROUPS, BLOCK, 2 * BLOCK), jnp.float32)]
    operands = [qkv for qkv in qkvs for _ in range(5)]
    return pl.pallas_call(
        _attn_kernel,
        grid=(bsz, HEADS_PER_GROUP, seq // SPAN),
        in_specs=in_specs,
        out_specs=pl.BlockSpec((None, SPAN, HEAD_DIM), lambda b, h, n: (b, n, h)),
        out_shape=jax.ShapeDtypeStruct((bsz, seq, GROUP_WIDTH), bf),
        scratch_shapes=scratch,
        compiler_params=pltpu.CompilerParams(
            dimension_semantics=("arbitrary", "arbitrary", "arbitrary"), vmem_limit_bytes=VMEM_LIMIT),
        name="dilated_attn",
    )(*operands)


def _tail_kernel(x_ref, mod_ref, oa_ref, cbu_ref, sa_ref, sb_ref, wba_ref, wbc_ref, wo_ref, gm_ref,
                 wmi_ref, wmo_ref, gf_ref, out_ref):
    bf = jnp.bfloat16
    f32 = jnp.float32
    y_attn = jnp.dot(oa_ref[0], wba_ref[...], preferred_element_type=f32)
    y_conv = jnp.dot(cbu_ref[0], wbc_ref[...], preferred_element_type=f32)
    merged = sa_ref[0].astype(f32) * y_attn + sb_ref[0].astype(f32) * y_conv
    x1 = x_ref[0] + mod_ref[0, 2:3, :] * jnp.dot(merged.astype(bf), wo_ref[...],
                                                  preferred_element_type=f32)

    h2 = _rms_modulate(x1, gm_ref[...], mod_ref[0, 4:5, :], mod_ref[0, 3:4, :]).astype(bf)
    a = jnp.dot(h2, wmi_ref[...], preferred_element_type=f32)
    a = jnp.square(jnp.maximum(a, 0.0)).astype(bf)
    x2 = x1 + mod_ref[0, 5:6, :] * jnp.dot(a, wmo_ref[...], preferred_element_type=f32)

    ms = jnp.mean(x2 * x2, axis=-1, keepdims=True)
    out_ref[0] = x2 * lax.rsqrt(ms + EPS) * gf_ref[...]


def _tail_call(x, mod, o_attn, cbu, sa, sb, w_ba, w_bc, w_out, g_mlp, w_mi, w_mo, g_final, tm):
    bsz, seq, d = x.shape
    row_block = lambda width: pl.BlockSpec((1, tm, width), lambda b, s: (b, s, 0))
    return pl.pallas_call(
        _tail_kernel,
        grid=(bsz, seq // tm),
        in_specs=[row_block(d),
                  pl.BlockSpec((1, N_MOD, d), lambda b, s: (b, 0, 0)),
                  row_block(GROUP_WIDTH), row_block(d), row_block(d), row_block(d),
                  _resident(w_ba.shape), _resident(w_bc.shape), _resident(w_out.shape),
                  _resident((1, d)), _resident(w_mi.shape), _resident(w_mo.shape),
                  _resident((1, d))],
        out_specs=row_block(d),
        out_shape=jax.ShapeDtypeStruct((bsz, seq, d), jnp.float32),
        compiler_params=pltpu.CompilerParams(
            dimension_semantics=("arbitrary", "arbitrary"), vmem_limit_bytes=VMEM_LIMIT),
        name="mlp_tail",
    )(x, mod, o_attn, cbu, sa, sb, w_ba, w_bc, w_out, g_mlp, w_mi, w_mo, g_final)


def kernel(x, c, w_ada, b_ada, g_norm_mix, w_in, b_gate, conv_w, w_branch_attn, w_branch_conv,
           w_out, g_norm_mlp, w_mlp_in, w_mlp_out, g_norm_final):
    bsz, seq, d = x.shape
    depth = w_ada.shape[0]
    assert depth == 1, "the final norm is fused into the (single) layer's tail"
    bf = jnp.bfloat16
    tm = min(512, seq)
    for window, dilation in DILATED_GROUPS:
        assert window == dilation * BLOCK and tm % (dilation * 2 * SUBLANES) == 0
    l = 0
    mod = _ada_call(c, w_ada[l], b_ada[l]).reshape(bsz, N_MOD, d)
    *qkvs, cbu, sa, sb = _inproj_call(
        x, mod, g_norm_mix[l].reshape(1, d), w_in[l].astype(bf),
        b_gate[l].reshape(1, 2 * d), conv_w[l], tm)
    o_attn = _attn_call(qkvs)
    return _tail_call(x, mod, o_attn, cbu, sa, sb,
                      w_branch_attn[l].astype(bf), w_branch_conv[l].astype(bf),
                      w_out[l].astype(bf), g_norm_mlp[l].reshape(1, d),
                      w_mlp_in[l].astype(bf), w_mlp_out[l].astype(bf),
                      g_norm_final.reshape(1, d), tm)
```

```python
import math

import jax
import jax.numpy as jnp
from jax import lax
from jax.experimental import pallas as pl
from jax.experimental.pallas import tpu as pltpu

D_MODEL = 1024
HEAD_DIM = 128
HEADS_PER_GROUP = 4
DILATED_GROUPS = ((128, 1), (512, 4), (2048, 16))
N_GROUPS = len(DILATED_GROUPS)
N_ATTN_HEADS = N_GROUPS * HEADS_PER_GROUP
ATTN_WIDTH = N_ATTN_HEADS * HEAD_DIM
GROUP_WIDTH = HEADS_PER_GROUP * HEAD_DIM
BLOCK = 128
SPAN = max(w for w, _ in DILATED_GROUPS)
FAR = N_GROUPS - 1
FAR_PITCH = SPAN // DILATED_GROUPS[FAR][1] + 8
MERGE_ROWS = 16
CONV_K = 3
D_FF = 4 * D_MODEL
N_MOD = 6
QKV_WIDTH = 3 * ATTN_WIDTH
IN_COLS = QKV_WIDTH + 3 * D_MODEL + 2 * D_MODEL
EPS = 1e-6
NEG_INF = -1e30
LOG2E = math.log2(math.e)
LN2 = math.log(2.0)
LANES = 128
SUBLANES = 8
VMEM_LIMIT = 56 * 1024 * 1024

_CB0 = QKV_WIDTH
_CC0 = _CB0 + D_MODEL
_CX0 = _CC0 + D_MODEL
_GA0 = _CX0 + D_MODEL
_GB0 = _GA0 + D_MODEL


def _resident(shape):
    zeros = (0,) * len(shape)
    return pl.BlockSpec(shape, lambda *_: zeros, pipeline_mode=pl.Buffered(1))


def _rms_modulate(xf, g, scale, shift):
    ms = jnp.mean(xf * xf, axis=-1, keepdims=True)
    return (xf * lax.rsqrt(ms + EPS) * g) * (1.0 + scale) + shift


def _ada_kernel(c_ref, w_ref, b_ref, o_ref):
    c = c_ref[...]
    ca = c * jax.nn.sigmoid(c)
    o_ref[...] = jnp.dot(ca, w_ref[...], preferred_element_type=jnp.float32,
                         precision=lax.Precision.HIGHEST) + b_ref[...]


def _ada_call(c, w_ada, b_ada):
    bsz, d = c.shape
    n = w_ada.shape[1]
    return pl.pallas_call(
        _ada_kernel,
        grid=(n // d,),
        in_specs=[pl.BlockSpec((bsz, d), lambda j: (0, 0)),
                  pl.BlockSpec((d, d), lambda j: (0, j)),
                  pl.BlockSpec((1, d), lambda j: (0, j))],
        out_specs=pl.BlockSpec((bsz, d), lambda j: (0, j)),
        out_shape=jax.ShapeDtypeStruct((bsz, n), jnp.float32),
        name="ada_mod",
    )(c, w_ada, b_ada.reshape(1, n))


def _inproj_kernel(x_ref, mod_ref, g_ref, w_ref, bg_ref, cw_ref,
                   qkv0_ref, qkv1_ref, qkv2_ref, cbu_ref, sa_ref, sb_ref,
                   h_scr, carry_scr, slab_scr):
    tm = x_ref.shape[1]
    carry_in = carry_scr.at[(pl.program_id(1) + 1) % 2]
    carry_out = carry_scr.at[pl.program_id(1) % 2]

    @pl.when(pl.program_id(1) == 0)
    def _():
        carry_in[...] = jnp.zeros_like(carry_in)

    h = _rms_modulate(x_ref[0], g_ref[...], mod_ref[0, 1:2, :], mod_ref[0, 0:1, :])
    h_scr[0] = h.astype(jnp.bfloat16)
    n_slab = h.shape[1] // LANES
    for c in range(n_slab):
        slab_scr[c] = h[:, c * LANES:(c + 1) * LANES]
    for gi, (_, d) in enumerate(DILATED_GROUPS):
        if d == 1:
            continue
        for r in range(d):
            for c in range(n_slab):
                h_scr[gi, r * (tm // d):(r + 1) * (tm // d), c * LANES:(c + 1) * LANES] = (
                    slab_scr[c, pl.ds(r, tm // d, stride=d), :].astype(jnp.bfloat16))

    def proj(c0, width, gi=0):
        return jnp.dot(h_scr[gi], w_ref[:, c0:c0 + width], preferred_element_type=jnp.float32)

    for gi, (out_ref, (_, d)) in enumerate(zip((qkv0_ref, qkv1_ref, qkv2_ref), DILATED_GROUPS)):
        for which in range(3):
            res = proj(which * ATTN_WIDTH + gi * GROUP_WIDTH, GROUP_WIDTH, gi)
            if which == 0:
                res = res * (HEAD_DIM ** -0.5 * LOG2E)
            o0 = which * GROUP_WIDTH
            for r in range(d):
                out_ref[0, r, :, o0:o0 + GROUP_WIDTH] = (
                    res[r * (tm // d):(r + 1) * (tm // d), :].astype(out_ref.dtype))

    p = proj(_CC0, D_MODEL) * proj(_CX0, D_MODEL)
    row = lax.broadcasted_iota(jnp.int32, p.shape, 0)
    prev1 = carry_in[SUBLANES - 1:SUBLANES, :]
    prev2 = carry_in[SUBLANES - 2:SUBLANES - 1, :]
    p1 = jnp.where(row == 0, prev1, pltpu.roll(p, 1, 0))
    p2 = jnp.where(row == 0, prev2, jnp.where(row == 1, prev1, pltpu.roll(p, 2, 0)))
    u = cw_ref[0:1, :] * p2 + cw_ref[1:2, :] * p1 + cw_ref[2:3, :] * p
    carry_out[...] = p[tm - SUBLANES:, :]
    cbu_ref[0] = (proj(_CB0, D_MODEL) * u).astype(cbu_ref.dtype)

    sa_ref[0] = jax.nn.sigmoid(proj(_GA0, D_MODEL) + bg_ref[:, :D_MODEL]).astype(sa_ref.dtype)
    sb_ref[0] = jax.nn.sigmoid(proj(_GB0, D_MODEL) + bg_ref[:, D_MODEL:]).astype(sb_ref.dtype)


def _inproj_call(x, mod, g_mix, w_in, b_gate, conv_w, tm):
    bsz, seq, d = x.shape
    bf = jnp.bfloat16
    row_block = lambda width: pl.BlockSpec((1, tm, width), lambda b, s: (b, s, 0))
    qkv_specs, qkv_shapes = [], []
    for _, dil in DILATED_GROUPS:
        qkv_specs.append(pl.BlockSpec((1, dil, tm // dil, 3 * GROUP_WIDTH), lambda b, s: (b, 0, s, 0)))
        qkv_shapes.append(jax.ShapeDtypeStruct((bsz, dil, seq // dil, 3 * GROUP_WIDTH), bf))
    return pl.pallas_call(
        _inproj_kernel,
        grid=(bsz, seq // tm),
        in_specs=[row_block(d),
                  pl.BlockSpec((1, N_MOD, d), lambda b, s: (b, 0, 0)),
                  _resident((1, d)),
                  _resident((d, IN_COLS)),
                  _resident((1, 2 * d)),
                  _resident((CONV_K, d))],
        out_specs=qkv_specs + [row_block(d), row_block(d), row_block(d)],
        out_shape=qkv_shapes + [jax.ShapeDtypeStruct((bsz, seq, d), bf)] * 3,
        scratch_shapes=[pltpu.VMEM((N_GROUPS, tm, d), bf), pltpu.VMEM((2, SUBLANES, d), jnp.float32),
                        pltpu.VMEM((d // LANES, tm, LANES), jnp.float32)],
        compiler_params=pltpu.CompilerParams(
            dimension_semantics=("arbitrary", "arbitrary"), vmem_limit_bytes=VMEM_LIMIT),
        name="inproj",
    )(x, mod, g_mix, w_in, b_gate, conv_w)


def _alibi_slope(head):
    return 2.0 ** (-8.0 * (head + 1) / N_ATTN_HEADS)


def _attn_kernel(*refs):
    ins, (out_ref,), scr = refs[:5 * N_GROUPS], refs[5 * N_GROUPS:5 * N_GROUPS + 1], refs[5 * N_GROUPS + 1:]
    q_refs, k_refs, v_refs, kp_refs, vp_refs = (ins[i::5] for i in range(5))
    ks, vs = scr[0:2 * N_GROUPS:2], scr[1:2 * N_GROUPS:2]
    near_scr, far_scr, bias_scr = scr[2 * N_GROUPS:]
    head = pl.program_id(1)
    chunk = pl.program_id(2)
    first_chunk = chunk == 0

    @pl.when((pl.program_id(0) == 0) & (head == 0) & first_chunk)
    def _():
        for v_scr in vs:
            v_scr[:, :, HEAD_DIM:] = jnp.ones(v_scr.shape[:2] + (HEAD_DIM,), v_scr.dtype)
        qi = lax.broadcasted_iota(jnp.int32, (BLOCK, 2 * BLOCK), 0)
        kj = lax.broadcasted_iota(jnp.int32, (BLOCK, 2 * BLOCK), 1)
        delta = BLOCK + qi - kj
        in_window = (delta >= 0) & (delta <= BLOCK)
        for gi, (_, d) in enumerate(DILATED_GROUPS):
            dist = (delta * d).astype(jnp.float32)
            for hh in range(HEADS_PER_GROUP):
                slope = _alibi_slope(gi * HEADS_PER_GROUP + hh) * LOG2E
                bias_scr[0, hh, gi] = jnp.where(in_window, -slope * dist, NEG_INF)
                bias_scr[1, hh, gi] = jnp.where(in_window & (kj >= BLOCK), -slope * dist, NEG_INF)

    for gi in range(N_GROUPS):
        ks[gi][:, 0:BLOCK, :] = kp_refs[gi][...]
        vs[gi][:, 0:BLOCK, 0:HEAD_DIM] = vp_refs[gi][...]
        ks[gi][:, BLOCK:, :] = k_refs[gi][...]
        vs[gi][:, BLOCK:, 0:HEAD_DIM] = v_refs[gi][...]

    first_flag = first_chunk.astype(jnp.int32)

    def body(gi, r, j):
        d = DILATED_GROUPS[gi][1]
        q = q_refs[gi][r, j * BLOCK:(j + 1) * BLOCK, :]
        kk = ks[gi][r, j * BLOCK:(j + 2) * BLOCK, :]
        vv = vs[gi][r, j * BLOCK:(j + 2) * BLOCK, :]
        bias = bias_scr[first_flag if j == 0 else 0, head, gi]
        s = lax.dot_general(q, kk, (((1,), (1,)), ((), ())),
                            preferred_element_type=jnp.float32) + bias
        m = jnp.max(s, axis=-1, keepdims=True)
        p = jnp.exp2(s - m).astype(vv.dtype)
        ov = jnp.dot(p, vv, preferred_element_type=jnp.float32)
        parts = (ov[:, :HEAD_DIM], jnp.broadcast_to(m, (BLOCK, HEAD_DIM)), ov[:, HEAD_DIM:])
        if gi == FAR:
            rows = pl.ds(r * FAR_PITCH + j * BLOCK, BLOCK)
            for kind, val in enumerate(parts):
                far_scr[kind, rows, :] = val
        else:
            start = j * BLOCK * d + r
            rows = pl.ds(start, BLOCK, stride=d) if d > 1 else pl.ds(start, BLOCK)
            for kind, val in enumerate(parts):
                near_scr[gi, kind, rows, :] = val

    d_far = DILATED_GROUPS[FAR][1]

    def far_rows(kind, row0):
        pieces = []
        for t in range(row0, row0 + MERGE_ROWS, SUBLANES):
            pieces.append(far_scr[kind, pl.ds((t % d_far) * FAR_PITCH + t // d_far, SUBLANES,
                                              stride=FAR_PITCH), :])
        return jnp.concatenate(pieces, axis=0)

    def merge(row0):
        rows = pl.ds(row0, MERGE_ROWS)
        o = [near_scr[0, 0, rows, :], near_scr[1, 0, rows, :], far_rows(0, row0)]
        m = [near_scr[0, 1, rows, :], near_scr[1, 1, rows, :], far_rows(1, row0)]
        den = [near_scr[0, 2, rows, :], near_scr[1, 2, rows, :], far_rows(2, row0)]
        top = jnp.maximum(jnp.maximum(m[0], m[1]), m[2])
        a = [jnp.exp2(mg - top) for mg in m]
        num = a[0] * o[0] + a[1] * o[1] + a[2] * o[2]
        tot = a[0] * den[0] + a[1] * den[1] + a[2] * den[2]
        out_ref[rows, :] = (num / tot).astype(out_ref.dtype)

    sub = [SPAN // d // BLOCK for _, d in DILATED_GROUPS]
    for r in range(DILATED_GROUPS[2][1]):
        for j in range(sub[2]):
            body(2, r, j)
    stretch = SPAN // sub[1]
    for j1 in range(sub[1]):
        for r in range(DILATED_GROUPS[1][1]):
            body(1, r, j1)
        for j0 in range(j1 * sub[0] // sub[1], (j1 + 1) * sub[0] // sub[1]):
            body(0, 0, j0)
        for row0 in range(j1 * stretch, (j1 + 1) * stretch, MERGE_ROWS):
            merge(row0)


def _attn_call(qkvs):
    bsz = qkvs[0].shape[0]
    seq = qkvs[0].shape[1] * qkvs[0].shape[2]
    assert seq % SPAN == 0
    bf = qkvs[0].dtype
    in_specs, scratch = [], []
    for _, d in DILATED_GROUPS:
        nb = SPAN // d // BLOCK
        for which in range(3):
            in_specs.append(pl.BlockSpec(
                (None, d, SPAN // d, HEAD_DIM),
                lambda b, h, n, which=which: (b, 0, n, which * HEADS_PER_GROUP + h)))
        for which in (1, 2):
            in_specs.append(pl.BlockSpec(
                (None, d, BLOCK, HEAD_DIM),
                lambda b, h, n, which=which, nb=nb: (b, 0, jnp.maximum(n * nb - 1, 0),
                                                     which * HEADS_PER_GROUP + h)))
        scratch += [pltpu.VMEM((d, SPAN // d + BLOCK, HEAD_DIM), bf),
                    pltpu.VMEM((d, SPAN // d + BLOCK, 2 * HEAD_DIM), bf)]
    scratch += [pltpu.VMEM((FAR, 3, SPAN, HEAD_DIM), jnp.float32),
                pltpu.VMEM((3, DILATED_GROUPS[FAR][1] * FAR_PITCH, HEAD_DIM), jnp.float32),
                pltpu.VMEM((2, HEADS_PER_GROUP, N_GROUPS, BLOCK, 2 * BLOCK), jnp.float32)]
    operands = [qkv for qkv in qkvs for _ in range(5)]
    return pl.pallas_call(
        _attn_kernel,
        grid=(bsz, HEADS_PER_GROUP, seq // SPAN),
        in_specs=in_specs,
        out_specs=pl.BlockSpec((None, SPAN, HEAD_DIM), lambda b, h, n: (b, n, h)),
        out_shape=jax.ShapeDtypeStruct((bsz, seq, GROUP_WIDTH), bf),
        scratch_shapes=scratch,
        compiler_params=pltpu.CompilerParams(
            dimension_semantics=("arbitrary", "arbitrary", "arbitrary"), vmem_limit_bytes=VMEM_LIMIT),
        name="dilated_attn",
    )(*operands)


def _tail_kernel(x_ref, mod_ref, oa_ref, cbu_ref, sa_ref, sb_ref, wba_ref, wbc_ref, wo_ref, gm_ref,
                 wmi_ref, wmo_ref, gf_ref, out_ref):
    bf = jnp.bfloat16
    f32 = jnp.float32
    y_attn = jnp.dot(oa_ref[0], wba_ref[...], preferred_element_type=f32)
    y_conv = jnp.dot(cbu_ref[0], wbc_ref[...], preferred_element_type=f32)
    merged = sa_ref[0].astype(f32) * y_attn + sb_ref[0].astype(f32) * y_conv
    x1 = x_ref[0] + mod_ref[0, 2:3, :] * jnp.dot(merged.astype(bf), wo_ref[...],
                                                  preferred_element_type=f32)

    h2 = _rms_modulate(x1, gm_ref[...], mod_ref[0, 4:5, :], mod_ref[0, 3:4, :]).astype(bf)
    a = jnp.dot(h2, wmi_ref[...], preferred_element_type=f32)
    a = jnp.square(jnp.maximum(a, 0.0)).astype(bf)
    x2 = x1 + mod_ref[0, 5:6, :] * jnp.dot(a, wmo_ref[...], preferred_element_type=f32)

    ms = jnp.mean(x2 * x2, axis=-1, keepdims=True)
    out_ref[0] = x2 * lax.rsqrt(ms + EPS) * gf_ref[...]


def _tail_call(x, mod, o_attn, cbu, sa, sb, w_ba, w_bc, w_out, g_mlp, w_mi, w_mo, g_final, tm):
    bsz, seq, d = x.shape
    row_block = lambda width: pl.BlockSpec((1, tm, width), lambda b, s: (b, s, 0))
    return pl.pallas_call(
        _tail_kernel,
        grid=(bsz, seq // tm),
        in_specs=[row_block(d),
                  pl.BlockSpec((1, N_MOD, d), lambda b, s: (b, 0, 0)),
                  row_block(GROUP_WIDTH), row_block(d), row_block(d), row_block(d),
                  _resident(w_ba.shape), _resident(w_bc.shape), _resident(w_out.shape),
                  _resident((1, d)), _resident(w_mi.shape), _resident(w_mo.shape),
                  _resident((1, d))],
        out_specs=row_block(d),
        out_shape=jax.ShapeDtypeStruct((bsz, seq, d), jnp.float32),
        compiler_params=pltpu.CompilerParams(
            dimension_semantics=("arbitrary", "arbitrary"), vmem_limit_bytes=VMEM_LIMIT),
        name="mlp_tail",
    )(x, mod, o_attn, cbu, sa, sb, w_ba, w_bc, w_out, g_mlp, w_mi, w_mo, g_final)


def kernel(x, c, w_ada, b_ada, g_norm_mix, w_in, b_gate, conv_w, w_branch_attn, w_branch_conv,
           w_out, g_norm_mlp, w_mlp_in, w_mlp_out, g_norm_final):
    bsz, seq, d = x.shape
    depth = w_ada.shape[0]
    assert depth == 1, "the final norm is fused into the (single) layer's tail"
    bf = jnp.bfloat16
    tm = min(512, seq)
    for window, dilation in DILATED_GROUPS:
        assert window == dilation * BLOCK and tm % (dilation * 2 * SUBLANES) == 0
    l = 0
    mod = _ada_call(c, w_ada[l], b_ada[l]).reshape(bsz, N_MOD, d)
    *qkvs, cbu, sa, sb = _inproj_call(
        x, mod, g_norm_mix[l].reshape(1, d), w_in[l].astype(bf),
        b_gate[l].reshape(1, 2 * d), conv_w[l], tm)
    o_attn = _attn_call(qkvs)
    return _tail_call(x, mod, o_attn, cbu, sa, sb,
                      w_branch_attn[l].astype(bf), w_branch_conv[l].astype(bf),
                      w_out[l].astype(bf), g_norm_mlp[l].reshape(1, d),
                      w_mlp_in[l].astype(bf), w_mlp_out[l].astype(bf),
                      g_norm_final.reshape(1, d), tm)
```

```python
import math

import jax
import jax.numpy as jnp
from jax import lax
from jax.experimental import pallas as pl
from jax.experimental.pallas import tpu as pltpu

D_MODEL = 1024
HEAD_DIM = 128
HEADS_PER_GROUP = 4
DILATED_GROUPS = ((128, 1), (512, 4), (2048, 16))
N_GROUPS = len(DILATED_GROUPS)
N_ATTN_HEADS = N_GROUPS * HEADS_PER_GROUP
ATTN_WIDTH = N_ATTN_HEADS * HEAD_DIM
GROUP_WIDTH = HEADS_PER_GROUP * HEAD_DIM
BLOCK = 128
SPAN = max(w for w, _ in DILATED_GROUPS)
FAR = N_GROUPS - 1
FAR_PITCH = SPAN // DILATED_GROUPS[FAR][1] + 8
CONV_K = 3
D_FF = 4 * D_MODEL
N_MOD = 6
QKV_WIDTH = 3 * ATTN_WIDTH
IN_COLS = QKV_WIDTH + 3 * D_MODEL + 2 * D_MODEL
EPS = 1e-6
NEG_INF = -1e30
LOG2E = math.log2(math.e)
LN2 = math.log(2.0)
LANES = 128
SUBLANES = 8
VMEM_LIMIT = 56 * 1024 * 1024

_CB0 = QKV_WIDTH
_CC0 = _CB0 + D_MODEL
_CX0 = _CC0 + D_MODEL
_GA0 = _CX0 + D_MODEL
_GB0 = _GA0 + D_MODEL


def _resident(shape):
    zeros = (0,) * len(shape)
    return pl.BlockSpec(shape, lambda *_: zeros, pipeline_mode=pl.Buffered(1))


def _rms_modulate(xf, g, scale, shift):
    ms = jnp.mean(xf * xf, axis=-1, keepdims=True)
    return (xf * lax.rsqrt(ms + EPS) * g) * (1.0 + scale) + shift


def _ada_kernel(c_ref, w_ref, b_ref, o_ref):
    c = c_ref[...]
    ca = c * jax.nn.sigmoid(c)
    o_ref[...] = jnp.dot(ca, w_ref[...], preferred_element_type=jnp.float32,
                         precision=lax.Precision.HIGHEST) + b_ref[...]


def _ada_call(c, w_ada, b_ada):
    bsz, d = c.shape
    n = w_ada.shape[1]
    return pl.pallas_call(
        _ada_kernel,
        grid=(n // d,),
        in_specs=[pl.BlockSpec((bsz, d), lambda j: (0, 0)),
                  pl.BlockSpec((d, d), lambda j: (0, j)),
                  pl.BlockSpec((1, d), lambda j: (0, j))],
        out_specs=pl.BlockSpec((bsz, d), lambda j: (0, j)),
        out_shape=jax.ShapeDtypeStruct((bsz, n), jnp.float32),
        name="ada_mod",
    )(c, w_ada, b_ada.reshape(1, n))


def _inproj_kernel(x_ref, mod_ref, g_ref, w_ref, bg_ref, cw_ref,
                   qkv0_ref, qkv1_ref, qkv2_ref, cbu_ref, sa_ref, sb_ref,
                   h_scr, carry_scr, slab_scr):
    tm = x_ref.shape[1]
    carry_in = carry_scr.at[(pl.program_id(1) + 1) % 2]
    carry_out = carry_scr.at[pl.program_id(1) % 2]

    @pl.when(pl.program_id(1) == 0)
    def _():
        carry_in[...] = jnp.zeros_like(carry_in)

    h = _rms_modulate(x_ref[0], g_ref[...], mod_ref[0, 1:2, :], mod_ref[0, 0:1, :])
    h_scr[0] = h.astype(jnp.bfloat16)
    n_slab = h.shape[1] // LANES
    for c in range(n_slab):
        slab_scr[c] = h[:, c * LANES:(c + 1) * LANES]
    for gi, (_, d) in enumerate(DILATED_GROUPS):
        if d == 1:
            continue
        for r in range(d):
            for c in range(n_slab):
                h_scr[gi, r * (tm // d):(r + 1) * (tm // d), c * LANES:(c + 1) * LANES] = (
                    slab_scr[c, pl.ds(r, tm // d, stride=d), :].astype(jnp.bfloat16))

    def proj(c0, width, gi=0):
        return jnp.dot(h_scr[gi], w_ref[:, c0:c0 + width], preferred_element_type=jnp.float32)

    p = proj(_CC0, D_MODEL) * proj(_CX0, D_MODEL)
    row = lax.broadcasted_iota(jnp.int32, p.shape, 0)
    prev1 = carry_in[SUBLANES - 1:SUBLANES, :]
    prev2 = carry_in[SUBLANES - 2:SUBLANES - 1, :]
    p1 = jnp.where(row == 0, prev1, pltpu.roll(p, 1, 0))
    p2 = jnp.where(row == 0, prev2, jnp.where(row == 1, prev1, pltpu.roll(p, 2, 0)))
    u = cw_ref[0:1, :] * p2 + cw_ref[1:2, :] * p1 + cw_ref[2:3, :] * p
    carry_out[...] = p[tm - SUBLANES:, :]
    cbu_ref[0] = (proj(_CB0, D_MODEL) * u).astype(cbu_ref.dtype)

    sa_ref[0] = jax.nn.sigmoid(proj(_GA0, D_MODEL) + bg_ref[:, :D_MODEL]).astype(sa_ref.dtype)
    sb_ref[0] = jax.nn.sigmoid(proj(_GB0, D_MODEL) + bg_ref[:, D_MODEL:]).astype(sb_ref.dtype)

    for gi, (out_ref, (_, d)) in enumerate(zip((qkv0_ref, qkv1_ref, qkv2_ref), DILATED_GROUPS)):
        for which in range(3):
            res = proj(which * ATTN_WIDTH + gi * GROUP_WIDTH, GROUP_WIDTH, gi)
            if which == 0:
                res = res * (HEAD_DIM ** -0.5 * LOG2E)
            o0 = which * GROUP_WIDTH
            for r in range(d):
                out_ref[0, r, :, o0:o0 + GROUP_WIDTH] = (
                    res[r * (tm // d):(r + 1) * (tm // d), :].astype(out_ref.dtype))


def _inproj_call(x, mod, g_mix, w_in, b_gate, conv_w, tm):
    bsz, seq, d = x.shape
    bf = jnp.bfloat16
    row_block = lambda width: pl.BlockSpec((1, tm, width), lambda b, s: (b, s, 0))
    qkv_specs, qkv_shapes = [], []
    for _, dil in DILATED_GROUPS:
        qkv_specs.append(pl.BlockSpec((1, dil, tm // dil, 3 * GROUP_WIDTH), lambda b, s: (b, 0, s, 0)))
        qkv_shapes.append(jax.ShapeDtypeStruct((bsz, dil, seq // dil, 3 * GROUP_WIDTH), bf))
    return pl.pallas_call(
        _inproj_kernel,
        grid=(bsz, seq // tm),
        in_specs=[row_block(d),
                  pl.BlockSpec((1, N_MOD, d), lambda b, s: (b, 0, 0)),
                  _resident((1, d)),
                  _resident((d, IN_COLS)),
                  _resident((1, 2 * d)),
                  _resident((CONV_K, d))],
        out_specs=qkv_specs + [row_block(d), row_block(d), row_block(d)],
        out_shape=qkv_shapes + [jax.ShapeDtypeStruct((bsz, seq, d), bf)] * 3,
        scratch_shapes=[pltpu.VMEM((N_GROUPS, tm, d), bf), pltpu.VMEM((2, SUBLANES, d), jnp.float32),
                        pltpu.VMEM((d // LANES, tm, LANES), jnp.float32)],
        compiler_params=pltpu.CompilerParams(
            dimension_semantics=("arbitrary", "arbitrary"), vmem_limit_bytes=VMEM_LIMIT),
        name="inproj",
    )(x, mod, g_mix, w_in, b_gate, conv_w)


def _alibi_slope(head):
    return 2.0 ** (-8.0 * (head + 1) / N_ATTN_HEADS)


def _attn_kernel(*refs):
    ins, (out_ref,), scr = refs[:5 * N_GROUPS], refs[5 * N_GROUPS:5 * N_GROUPS + 1], refs[5 * N_GROUPS + 1:]
    q_refs, k_refs, v_refs, kp_refs, vp_refs = (ins[i::5] for i in range(5))
    ks, vs = scr[0:2 * N_GROUPS:2], scr[1:2 * N_GROUPS:2]
    mid_scr, far_scr, bias_scr = scr[2 * N_GROUPS:]
    head = pl.program_id(1)
    chunk = pl.program_id(2)
    first_chunk = chunk == 0

    @pl.when((pl.program_id(0) == 0) & (head == 0) & first_chunk)
    def _():
        for v_scr in vs:
            v_scr[:, :, HEAD_DIM:] = jnp.ones(v_scr.shape[:2] + (HEAD_DIM,), v_scr.dtype)
        qi = lax.broadcasted_iota(jnp.int32, (BLOCK, 2 * BLOCK), 0)
        kj = lax.broadcasted_iota(jnp.int32, (BLOCK, 2 * BLOCK), 1)
        delta = BLOCK + qi - kj
        in_window = (delta >= 0) & (delta <= BLOCK)
        for gi, (_, d) in enumerate(DILATED_GROUPS):
            dist = (delta * d).astype(jnp.float32)
            for hh in range(HEADS_PER_GROUP):
                slope = _alibi_slope(gi * HEADS_PER_GROUP + hh) * LOG2E
                bias_scr[0, hh, gi] = jnp.where(in_window, -slope * dist, NEG_INF)
                bias_scr[1, hh, gi] = jnp.where(in_window & (kj >= BLOCK), -slope * dist, NEG_INF)

    for gi in range(N_GROUPS):
        ks[gi][:, 0:BLOCK, :] = kp_refs[gi][...]
        vs[gi][:, 0:BLOCK, 0:HEAD_DIM] = vp_refs[gi][...]
        ks[gi][:, BLOCK:, :] = k_refs[gi][...]
        vs[gi][:, BLOCK:, 0:HEAD_DIM] = v_refs[gi][...]

    first_flag = first_chunk.astype(jnp.int32)

    def partials(gi, r, j):
        q = q_refs[gi][r, j * BLOCK:(j + 1) * BLOCK, :]
        kk = ks[gi][r, j * BLOCK:(j + 2) * BLOCK, :]
        vv = vs[gi][r, j * BLOCK:(j + 2) * BLOCK, :]
        bias = bias_scr[first_flag if j == 0 else 0, head, gi]
        s = lax.dot_general(q, kk, (((1,), (1,)), ((), ())),
                            preferred_element_type=jnp.float32) + bias
        m = jnp.max(s, axis=-1, keepdims=True)
        p = jnp.exp2(s - m).astype(vv.dtype)
        ov = jnp.dot(p, vv, preferred_element_type=jnp.float32)
        return ov[:, :HEAD_DIM], jnp.broadcast_to(m, (BLOCK, HEAD_DIM)), ov[:, HEAD_DIM:]

    d_mid, d_far = DILATED_GROUPS[1][1], DILATED_GROUPS[FAR][1]

    def far_rows(kind, row0):
        pieces = []
        for t in range(row0, row0 + BLOCK, SUBLANES):
            pieces.append(far_scr[kind, pl.ds((t % d_far) * FAR_PITCH + t // d_far, SUBLANES,
                                              stride=FAR_PITCH), :])
        return jnp.concatenate(pieces, axis=0)

    sub = [SPAN // d // BLOCK for _, d in DILATED_GROUPS]
    for r in range(d_far):
        for j in range(sub[FAR]):
            rows = pl.ds(r * FAR_PITCH + j * BLOCK, BLOCK)
            for kind, val in enumerate(partials(FAR, r, j)):
                far_scr[kind, rows, :] = val
    for j in range(sub[1]):
        for r in range(d_mid):
            rows = pl.ds(j * BLOCK * d_mid + r, BLOCK, stride=d_mid)
            for kind, val in enumerate(partials(1, r, j)):
                mid_scr[kind, rows, :] = val
    for j in range(sub[0]):
        rows = pl.ds(j * BLOCK, BLOCK)
        o = [partials(0, 0, j), tuple(mid_scr[kind, rows, :] for kind in range(3)),
             tuple(far_rows(kind, j * BLOCK) for kind in range(3))]
        top = jnp.maximum(jnp.maximum(o[0][1], o[1][1]), o[2][1])
        a = [jnp.exp2(part[1] - top) for part in o]
        num = a[0] * o[0][0] + a[1] * o[1][0] + a[2] * o[2][0]
        tot = a[0] * o[0][2] + a[1] * o[1][2] + a[2] * o[2][2]
        out_ref[rows, :] = (num / tot).astype(out_ref.dtype)


def _attn_call(qkvs):
    bsz = qkvs[0].shape[0]
    seq = qkvs[0].shape[1] * qkvs[0].shape[2]
    assert seq % SPAN == 0
    bf = qkvs[0].dtype
    in_specs, scratch = [], []
    for _, d in DILATED_GROUPS:
        nb = SPAN // d // BLOCK
        for which in range(3):
            in_specs.append(pl.BlockSpec(
                (None, d, SPAN // d, HEAD_DIM),
                lambda b, h, n, which=which: (b, 0, n, which * HEADS_PER_GROUP + h)))
        for which in (1, 2):
            in_specs.append(pl.BlockSpec(
                (None, d, BLOCK, HEAD_DIM),
                lambda b, h, n, which=which, nb=nb: (b, 0, jnp.maximum(n * nb - 1, 0),
                                                     which * HEADS_PER_GROUP + h)))
        scratch += [pltpu.VMEM((d, SPAN // d + BLOCK, HEAD_DIM), bf),
                    pltpu.VMEM((d, SPAN // d + BLOCK, 2 * HEAD_DIM), bf)]
    scratch += [pltpu.VMEM((3, SPAN, HEAD_DIM), jnp.float32),
                pltpu.VMEM((3, DILATED_GROUPS[FAR][1] * FAR_PITCH, HEAD_DIM), jnp.float32),
                pltpu.VMEM((2, HEADS_PER_GROUP, N_GROUPS, BLOCK, 2 * BLOCK), jnp.float32)]
    operands = [qkv for qkv in qkvs for _ in range(5)]
    return pl.pallas_call(
        _attn_kernel,
        grid=(bsz, HEADS_PER_GROUP, seq // SPAN),
        in_specs=in_specs,
        out_specs=pl.BlockSpec((None, SPAN, HEAD_DIM), lambda b, h, n: (b, n, h)),
        out_shape=jax.ShapeDtypeStruct((bsz, seq, GROUP_WIDTH), bf),
        scratch_shapes=scratch,
        compiler_params=pltpu.CompilerParams(
            dimension_semantics=("arbitrary", "arbitrary", "arbitrary"), vmem_limit_bytes=VMEM_LIMIT),
        name="dilated_attn",
    )(*operands)


def _tail_kernel(x_ref, mod_ref, oa_ref, cbu_ref, sa_ref, sb_ref, wba_ref, wbc_ref, wo_ref, gm_ref,
                 wmi_ref, wmo_ref, gf_ref, out_ref):
    bf = jnp.bfloat16
    f32 = jnp.float32
    y_attn = jnp.dot(oa_ref[0], wba_ref[...], preferred_element_type=f32)
    y_conv = jnp.dot(cbu_ref[0], wbc_ref[...], preferred_element_type=f32)
    merged = sa_ref[0].astype(f32) * y_attn + sb_ref[0].astype(f32) * y_conv
    x1 = x_ref[0] + mod_ref[0, 2:3, :] * jnp.dot(merged.astype(bf), wo_ref[...],
                                                  preferred_element_type=f32)

    h2 = _rms_modulate(x1, gm_ref[...], mod_ref[0, 4:5, :], mod_ref[0, 3:4, :]).astype(bf)
    a = jnp.dot(h2, wmi_ref[...], preferred_element_type=f32)
    a = jnp.square(jnp.maximum(a, 0.0)).astype(bf)
    x2 = x1 + mod_ref[0, 5:6, :] * jnp.dot(a, wmo_ref[...], preferred_element_type=f32)

    ms = jnp.mean(x2 * x2, axis=-1, keepdims=True)
    out_ref[0] = x2 * lax.rsqrt(ms + EPS) * gf_ref[...]


def _tail_call(x, mod, o_attn, cbu, sa, sb, w_ba, w_bc, w_out, g_mlp, w_mi, w_mo, g_final, tm):
    bsz, seq, d = x.shape
    row_block = lambda width: pl.BlockSpec((1, tm, width), lambda b, s: (b, s, 0))
    return pl.pallas_call(
        _tail_kernel,
        grid=(bsz, seq // tm),
        in_specs=[row_block(d),
                  pl.BlockSpec((1, N_MOD, d), lambda b, s: (b, 0, 0)),
                  row_block(GROUP_WIDTH), row_block(d), row_block(d), row_block(d),
                  _resident(w_ba.shape), _resident(w_bc.shape), _resident(w_out.shape),
                  _resident((1, d)), _resident(w_mi.shape), _resident(w_mo.shape),
                  _resident((1, d))],
        out_specs=row_block(d),
        out_shape=jax.ShapeDtypeStruct((bsz, seq, d), jnp.float32),
        compiler_params=pltpu.CompilerParams(
            dimension_semantics=("arbitrary", "arbitrary"), vmem_limit_bytes=VMEM_LIMIT),
        name="mlp_tail",
    )(x, mod, o_attn, cbu, sa, sb, w_ba, w_bc, w_out, g_mlp, w_mi, w_mo, g_final)


def kernel(x, c, w_ada, b_ada, g_norm_mix, w_in, b_gate, conv_w, w_branch_attn, w_branch_conv,
           w_out, g_norm_mlp, w_mlp_in, w_mlp_out, g_norm_final):
    bsz, seq, d = x.shape
    depth = w_ada.shape[0]
    assert depth == 1, "the final norm is fused into the (single) layer's tail"
    bf = jnp.bfloat16
    tm = min(512, seq)
    for window, dilation in DILATED_GROUPS:
        assert window == dilation * BLOCK and tm % (dilation * 2 * SUBLANES) == 0
    l = 0
    mod = _ada_call(c, w_ada[l], b_ada[l]).reshape(bsz, N_MOD, d)
    *qkvs, cbu, sa, sb = _inproj_call(
        x, mod, g_norm_mix[l].reshape(1, d), w_in[l].astype(bf),
        b_gate[l].reshape(1, 2 * d), conv_w[l], tm)
    o_attn = _attn_call(qkvs)
    return _tail_call(x, mod, o_attn, cbu, sa, sb,
                      w_branch_attn[l].astype(bf), w_branch_conv[l].astype(bf),
                      w_out[l].astype(bf), g_norm_mlp[l].reshape(1, d),
                      w_mlp_in[l].astype(bf), w_mlp_out[l].astype(bf),
                      g_norm_final.reshape(1, d), tm)
```

```python
import math

import jax
import jax.numpy as jnp
from jax import lax
from jax.experimental import pallas as pl
from jax.experimental.pallas import tpu as pltpu

D_MODEL = 1024
HEAD_DIM = 128
HEADS_PER_GROUP = 4
DILATED_GROUPS = ((128, 1), (512, 4), (2048, 16))
N_GROUPS = len(DILATED_GROUPS)
N_ATTN_HEADS = N_GROUPS * HEADS_PER_GROUP
ATTN_WIDTH = N_ATTN_HEADS * HEAD_DIM
GROUP_WIDTH = HEADS_PER_GROUP * HEAD_DIM
BLOCK = 128
SPAN = max(w for w, _ in DILATED_GROUPS)
FAR = N_GROUPS - 1
FAR_PITCH = SPAN // DILATED_GROUPS[FAR][1] + 8
CONV_K = 3
TAIL_SPLIT = 2
INPROJ_SPLIT = 2
D_FF = 4 * D_MODEL
N_MOD = 6
QKV_WIDTH = 3 * ATTN_WIDTH
IN_COLS = QKV_WIDTH + 3 * D_MODEL + 2 * D_MODEL
EPS = 1e-6
NEG_INF = -1e30
LOG2E = math.log2(math.e)
LN2 = math.log(2.0)
LANES = 128
SUBLANES = 8
VMEM_LIMIT = 56 * 1024 * 1024

_CB0 = QKV_WIDTH
_CC0 = _CB0 + D_MODEL
_CX0 = _CC0 + D_MODEL
_GA0 = _CX0 + D_MODEL
_GB0 = _GA0 + D_MODEL


def _resident(shape):
    zeros = (0,) * len(shape)
    return pl.BlockSpec(shape, lambda *_: zeros, pipeline_mode=pl.Buffered(1))


def _rms_modulate(xf, g, scale, shift):
    ms = jnp.mean(xf * xf, axis=-1, keepdims=True)
    return (xf * lax.rsqrt(ms + EPS) * g) * (1.0 + scale) + shift


def _ada_kernel(c_ref, w_ref, b_ref, o_ref):
    c = c_ref[...]
    ca = c * jax.nn.sigmoid(c)
    o_ref[...] = jnp.dot(ca, w_ref[...], preferred_element_type=jnp.float32,
                         precision=lax.Precision.HIGHEST) + b_ref[...]


def _ada_call(c, w_ada, b_ada):
    bsz, d = c.shape
    n = w_ada.shape[1]
    return pl.pallas_call(
        _ada_kernel,
        grid=(n // d,),
        in_specs=[pl.BlockSpec((bsz, d), lambda j: (0, 0)),
                  pl.BlockSpec((d, d), lambda j: (0, j)),
                  pl.BlockSpec((1, d), lambda j: (0, j))],
        out_specs=pl.BlockSpec((bsz, d), lambda j: (0, j)),
        out_shape=jax.ShapeDtypeStruct((bsz, n), jnp.float32),
        name="ada_mod",
    )(c, w_ada, b_ada.reshape(1, n))


def _inproj_kernel(x_ref, mod_ref, g_ref, w_ref, bg_ref, cw_ref,
                   qkv0_ref, qkv1_ref, qkv2_ref, cbu_ref, sa_ref, sb_ref,
                   h_scr, carry_scr, slab_scr):
    tm = x_ref.shape[1]
    tmh = tm // INPROJ_SPLIT
    starts = [i * tmh for i in range(INPROJ_SPLIT)]
    carry_in = carry_scr.at[(pl.program_id(1) + 1) % 2]
    carry_out = carry_scr.at[pl.program_id(1) % 2]

    @pl.when(pl.program_id(1) == 0)
    def _():
        carry_in[...] = jnp.zeros_like(carry_in)

    for s0 in starts:
        rows = pl.ds(s0, tmh)
        h = _rms_modulate(x_ref[0, rows, :], g_ref[...], mod_ref[0, 1:2, :], mod_ref[0, 0:1, :])
        h_scr[0, rows, :] = h.astype(jnp.bfloat16)
        n_slab = h.shape[1] // LANES
        for c in range(n_slab):
            slab_scr[c, rows, :] = h[:, c * LANES:(c + 1) * LANES]
        for gi, (_, d) in enumerate(DILATED_GROUPS):
            if d == 1:
                continue
            for r in range(d):
                for c in range(n_slab):
                    h_scr[gi, pl.ds(s0 + r * (tmh // d), tmh // d), c * LANES:(c + 1) * LANES] = (
                        slab_scr[c, pl.ds(s0 + r, tmh // d, stride=d), :].astype(jnp.bfloat16))

    def proj(s0, c0, width, gi=0):
        return jnp.dot(h_scr[gi, pl.ds(s0, tmh), :], w_ref[:, c0:c0 + width],
                       preferred_element_type=jnp.float32)

    prev1 = carry_in[SUBLANES - 1:SUBLANES, :]
    prev2 = carry_in[SUBLANES - 2:SUBLANES - 1, :]
    for s0 in starts:
        rows = pl.ds(s0, tmh)
        p = proj(s0, _CC0, D_MODEL) * proj(s0, _CX0, D_MODEL)
        row = lax.broadcasted_iota(jnp.int32, p.shape, 0)
        p1 = jnp.where(row == 0, prev1, pltpu.roll(p, 1, 0))
        p2 = jnp.where(row == 0, prev2, jnp.where(row == 1, prev1, pltpu.roll(p, 2, 0)))
        u = cw_ref[0:1, :] * p2 + cw_ref[1:2, :] * p1 + cw_ref[2:3, :] * p
        prev1, prev2 = p[tmh - 1:tmh, :], p[tmh - 2:tmh - 1, :]
        if s0 == starts[-1]:
            carry_out[...] = p[tmh - SUBLANES:, :]
        cbu_ref[0, rows, :] = (proj(s0, _CB0, D_MODEL) * u).astype(cbu_ref.dtype)

    for s0 in starts:
        rows = pl.ds(s0, tmh)
        sa_ref[0, rows, :] = jax.nn.sigmoid(
            proj(s0, _GA0, D_MODEL) + bg_ref[:, :D_MODEL]).astype(sa_ref.dtype)
        sb_ref[0, rows, :] = jax.nn.sigmoid(
            proj(s0, _GB0, D_MODEL) + bg_ref[:, D_MODEL:]).astype(sb_ref.dtype)

    for gi, (out_ref, (_, d)) in enumerate(zip((qkv0_ref, qkv1_ref, qkv2_ref), DILATED_GROUPS)):
        for which in range(3):
            for s0 in starts:
                res = proj(s0, which * ATTN_WIDTH + gi * GROUP_WIDTH, GROUP_WIDTH, gi)
                if which == 0:
                    res = res * (HEAD_DIM ** -0.5 * LOG2E)
                o0 = which * GROUP_WIDTH
                n = tmh // d
                for r in range(d):
                    out_ref[0, r, pl.ds(s0 // d, n), o0:o0 + GROUP_WIDTH] = (
                        res[r * n:(r + 1) * n, :].astype(out_ref.dtype))


def _inproj_call(x, mod, g_mix, w_in, b_gate, conv_w, tm):
    bsz, seq, d = x.shape
    bf = jnp.bfloat16
    row_block = lambda width: pl.BlockSpec((1, tm, width), lambda b, s: (b, s, 0))
    qkv_specs, qkv_shapes = [], []
    for _, dil in DILATED_GROUPS:
        qkv_specs.append(pl.BlockSpec((1, dil, tm // dil, 3 * GROUP_WIDTH), lambda b, s: (b, 0, s, 0)))
        qkv_shapes.append(jax.ShapeDtypeStruct((bsz, dil, seq // dil, 3 * GROUP_WIDTH), bf))
    return pl.pallas_call(
        _inproj_kernel,
        grid=(bsz, seq // tm),
        in_specs=[row_block(d),
                  pl.BlockSpec((1, N_MOD, d), lambda b, s: (b, 0, 0)),
                  _resident((1, d)),
                  _resident((d, IN_COLS)),
                  _resident((1, 2 * d)),
                  _resident((CONV_K, d))],
        out_specs=qkv_specs + [row_block(d), row_block(d), row_block(d)],
        out_shape=qkv_shapes + [jax.ShapeDtypeStruct((bsz, seq, d), bf)] * 3,
        scratch_shapes=[pltpu.VMEM((N_GROUPS, tm, d), bf), pltpu.VMEM((2, SUBLANES, d), jnp.float32),
                        pltpu.VMEM((d // LANES, tm, LANES), jnp.float32)],
        compiler_params=pltpu.CompilerParams(
            dimension_semantics=("arbitrary", "arbitrary"), vmem_limit_bytes=VMEM_LIMIT),
        name="inproj",
    )(x, mod, g_mix, w_in, b_gate, conv_w)


def _alibi_slope(head):
    return 2.0 ** (-8.0 * (head + 1) / N_ATTN_HEADS)


def _attn_kernel(*refs):
    ins, (out_ref,), scr = refs[:5 * N_GROUPS], refs[5 * N_GROUPS:5 * N_GROUPS + 1], refs[5 * N_GROUPS + 1:]
    q_refs, k_refs, v_refs, kp_refs, vp_refs = (ins[i::5] for i in range(5))
    ks, vs = scr[0:2 * N_GROUPS:2], scr[1:2 * N_GROUPS:2]
    mid_scr, far_scr, bias_scr = scr[2 * N_GROUPS:]
    head = pl.program_id(1)
    chunk = pl.program_id(2)
    first_chunk = chunk == 0

    @pl.when((pl.program_id(0) == 0) & (head == 0) & first_chunk)
    def _():
        for v_scr in vs:
            v_scr[:, :, HEAD_DIM:] = jnp.ones(v_scr.shape[:2] + (HEAD_DIM,), v_scr.dtype)
        qi = lax.broadcasted_iota(jnp.int32, (BLOCK, 2 * BLOCK), 0)
        kj = lax.broadcasted_iota(jnp.int32, (BLOCK, 2 * BLOCK), 1)
        delta = BLOCK + qi - kj
        in_window = (delta >= 0) & (delta <= BLOCK)
        for gi, (_, d) in enumerate(DILATED_GROUPS):
            dist = (delta * d).astype(jnp.float32)
            for hh in range(HEADS_PER_GROUP):
                slope = _alibi_slope(gi * HEADS_PER_GROUP + hh) * LOG2E
                bias_scr[0, hh, gi] = jnp.where(in_window, -slope * dist, NEG_INF)
                bias_scr[1, hh, gi] = jnp.where(in_window & (kj >= BLOCK), -slope * dist, NEG_INF)

    for gi in range(N_GROUPS):
        ks[gi][:, 0:BLOCK, :] = kp_refs[gi][...]
        vs[gi][:, 0:BLOCK, 0:HEAD_DIM] = vp_refs[gi][...]
        ks[gi][:, BLOCK:, :] = k_refs[gi][...]
        vs[gi][:, BLOCK:, 0:HEAD_DIM] = v_refs[gi][...]

    first_flag = first_chunk.astype(jnp.int32)

    def partials(gi, r, j):
        q = q_refs[gi][r, j * BLOCK:(j + 1) * BLOCK, :]
        kk = ks[gi][r, j * BLOCK:(j + 2) * BLOCK, :]
        vv = vs[gi][r, j * BLOCK:(j + 2) * BLOCK, :]
        bias = bias_scr[first_flag if j == 0 else 0, head, gi]
        s = lax.dot_general(q, kk, (((1,), (1,)), ((), ())),
                            preferred_element_type=jnp.float32) + bias
        m = jnp.max(s, axis=-1, keepdims=True)
        p = jnp.exp2(s - m).astype(vv.dtype)
        ov = jnp.dot(p, vv, preferred_element_type=jnp.float32)
        return ov[:, :HEAD_DIM], jnp.broadcast_to(m, (BLOCK, HEAD_DIM)), ov[:, HEAD_DIM:]

    d_mid, d_far = DILATED_GROUPS[1][1], DILATED_GROUPS[FAR][1]

    def far_rows(kind, row0):
        pieces = []
        for t in range(row0, row0 + BLOCK, SUBLANES):
            pieces.append(far_scr[kind, pl.ds((t % d_far) * FAR_PITCH + t // d_far, SUBLANES,
                                              stride=FAR_PITCH), :])
        return jnp.concatenate(pieces, axis=0)

    sub = [SPAN // d // BLOCK for _, d in DILATED_GROUPS]
    for r in range(d_far):
        for j in range(sub[FAR]):
            rows = pl.ds(r * FAR_PITCH + j * BLOCK, BLOCK)
            for kind, val in enumerate(partials(FAR, r, j)):
                far_scr[kind, rows, :] = val
    for j in range(sub[1]):
        for r in range(d_mid):
            rows = pl.ds(j * BLOCK * d_mid + r, BLOCK, stride=d_mid)
            for kind, val in enumerate(partials(1, r, j)):
                mid_scr[kind, rows, :] = val
    for j in range(sub[0]):
        rows = pl.ds(j * BLOCK, BLOCK)
        o = [partials(0, 0, j), tuple(mid_scr[kind, rows, :] for kind in range(3)),
             tuple(far_rows(kind, j * BLOCK) for kind in range(3))]
        top = jnp.maximum(jnp.maximum(o[0][1], o[1][1]), o[2][1])
        a = [jnp.exp2(part[1] - top) for part in o]
        num = a[0] * o[0][0] + a[1] * o[1][0] + a[2] * o[2][0]
        tot = a[0] * o[0][2] + a[1] * o[1][2] + a[2] * o[2][2]
        out_ref[rows, :] = (num / tot).astype(out_ref.dtype)


def _attn_call(qkvs):
    bsz = qkvs[0].shape[0]
    seq = qkvs[0].shape[1] * qkvs[0].shape[2]
    assert seq % SPAN == 0
    bf = qkvs[0].dtype
    in_specs, scratch = [], []
    for _, d in DILATED_GROUPS:
        nb = SPAN // d // BLOCK
        for which in range(3):
            in_specs.append(pl.BlockSpec(
                (None, d, SPAN // d, HEAD_DIM),
                lambda b, h, n, which=which: (b, 0, n, which * HEADS_PER_GROUP + h)))
        for which in (1, 2):
            in_specs.append(pl.BlockSpec(
                (None, d, BLOCK, HEAD_DIM),
                lambda b, h, n, which=which, nb=nb: (b, 0, jnp.maximum(n * nb - 1, 0),
                                                     which * HEADS_PER_GROUP + h)))
        scratch += [pltpu.VMEM((d, SPAN // d + BLOCK, HEAD_DIM), bf),
                    pltpu.VMEM((d, SPAN // d + BLOCK, 2 * HEAD_DIM), bf)]
    scratch += [pltpu.VMEM((3, SPAN, HEAD_DIM), jnp.float32),
                pltpu.VMEM((3, DILATED_GROUPS[FAR][1] * FAR_PITCH, HEAD_DIM), jnp.float32),
                pltpu.VMEM((2, HEADS_PER_GROUP, N_GROUPS, BLOCK, 2 * BLOCK), jnp.float32)]
    operands = [qkv for qkv in qkvs for _ in range(5)]
    return pl.pallas_call(
        _attn_kernel,
        grid=(bsz, HEADS_PER_GROUP, seq // SPAN),
        in_specs=in_specs,
        out_specs=pl.BlockSpec((None, SPAN, HEAD_DIM), lambda b, h, n: (b, n, h)),
        out_shape=jax.ShapeDtypeStruct((bsz, seq, GROUP_WIDTH), bf),
        scratch_shapes=scratch,
        compiler_params=pltpu.CompilerParams(
            dimension_semantics=("arbitrary", "arbitrary", "arbitrary"), vmem_limit_bytes=VMEM_LIMIT),
        name="dilated_attn",
    )(*operands)


def _tail_kernel(x_ref, mod_ref, oa_ref, cbu_ref, sa_ref, sb_ref, wba_ref, wbc_ref, wo_ref, gm_ref,
                 wmi_ref, wmo_ref, gf_ref, out_ref):
    bf = jnp.bfloat16
    f32 = jnp.float32
    tm = x_ref.shape[1]
    halves = tuple(pl.ds(i * (tm // TAIL_SPLIT), tm // TAIL_SPLIT) for i in range(TAIL_SPLIT))
    x1 = []
    for rows in halves:
        y_attn = jnp.dot(oa_ref[0, rows, :], wba_ref[...], preferred_element_type=f32)
        y_conv = jnp.dot(cbu_ref[0, rows, :], wbc_ref[...], preferred_element_type=f32)
        merged = sa_ref[0, rows, :].astype(f32) * y_attn + sb_ref[0, rows, :].astype(f32) * y_conv
        x1.append(x_ref[0, rows, :] + mod_ref[0, 2:3, :] * jnp.dot(
            merged.astype(bf), wo_ref[...], preferred_element_type=f32))
    act = []
    for xh in x1:
        h2 = _rms_modulate(xh, gm_ref[...], mod_ref[0, 4:5, :], mod_ref[0, 3:4, :]).astype(bf)
        a = jnp.dot(h2, wmi_ref[...], preferred_element_type=f32)
        act.append(jnp.square(jnp.maximum(a, 0.0)).astype(bf))
    for rows, xh, a in zip(halves, x1, act):
        x2 = xh + mod_ref[0, 5:6, :] * jnp.dot(a, wmo_ref[...], preferred_element_type=f32)
        ms = jnp.mean(x2 * x2, axis=-1, keepdims=True)
        out_ref[0, rows, :] = x2 * lax.rsqrt(ms + EPS) * gf_ref[...]


def _tail_call(x, mod, o_attn, cbu, sa, sb, w_ba, w_bc, w_out, g_mlp, w_mi, w_mo, g_final, tm):
    bsz, seq, d = x.shape
    row_block = lambda width: pl.BlockSpec((1, tm, width), lambda b, s: (b, s, 0))
    return pl.pallas_call(
        _tail_kernel,
        grid=(bsz, seq // tm),
        in_specs=[row_block(d),
                  pl.BlockSpec((1, N_MOD, d), lambda b, s: (b, 0, 0)),
                  row_block(GROUP_WIDTH), row_block(d), row_block(d), row_block(d),
                  _resident(w_ba.shape), _resident(w_bc.shape), _resident(w_out.shape),
                  _resident((1, d)), _resident(w_mi.shape), _resident(w_mo.shape),
                  _resident((1, d))],
        out_specs=row_block(d),
        out_shape=jax.ShapeDtypeStruct((bsz, seq, d), jnp.float32),
        compiler_params=pltpu.CompilerParams(
            dimension_semantics=("arbitrary", "arbitrary"), vmem_limit_bytes=VMEM_LIMIT),
        name="mlp_tail",
    )(x, mod, o_attn, cbu, sa, sb, w_ba, w_bc, w_out, g_mlp, w_mi, w_mo, g_final)


def kernel(x, c, w_ada, b_ada, g_norm_mix, w_in, b_gate, conv_w, w_branch_attn, w_branch_conv,
           w_out, g_norm_mlp, w_mlp_in, w_mlp_out, g_norm_final):
    bsz, seq, d = x.shape
    depth = w_ada.shape[0]
    assert depth == 1, "the final norm is fused into the (single) layer's tail"
    bf = jnp.bfloat16
    tm = min(512, seq)
    for window, dilation in DILATED_GROUPS:
        assert window == dilation * BLOCK and tm % (dilation * 2 * SUBLANES) == 0
    l = 0
    mod = _ada_call(c, w_ada[l], b_ada[l]).reshape(bsz, N_MOD, d)
    *qkvs, cbu, sa, sb = _inproj_call(
        x, mod, g_norm_mix[l].reshape(1, d), w_in[l].astype(bf),
        b_gate[l].reshape(1, 2 * d), conv_w[l], tm)
    o_attn = _attn_call(qkvs)
    return _tail_call(x, mod, o_attn, cbu, sa, sb,
                      w_branch_attn[l].astype(bf), w_branch_conv[l].astype(bf),
                      w_out[l].astype(bf), g_norm_mlp[l].reshape(1, d),
                      w_mlp_in[l].astype(bf), w_mlp_out[l].astype(bf),
                      g_norm_final.reshape(1, d), tm)
```

```python
import math

import jax
import jax.numpy as jnp
from jax import lax
from jax.experimental import pallas as pl
from jax.experimental.pallas import tpu as pltpu

D_MODEL = 1024
HEAD_DIM = 128
HEADS_PER_GROUP = 4
DILATED_GROUPS = ((128, 1), (512, 4), (2048, 16))
N_GROUPS = len(DILATED_GROUPS)
N_ATTN_HEADS = N_GROUPS * HEADS_PER_GROUP
ATTN_WIDTH = N_ATTN_HEADS * HEAD_DIM
GROUP_WIDTH = HEADS_PER_GROUP * HEAD_DIM
BLOCK = 128
SPAN = max(w for w, _ in DILATED_GROUPS)
FAR = N_GROUPS - 1
FAR_PITCH = SPAN // DILATED_GROUPS[FAR][1] + 8
CONV_K = 3
TAIL_SPLIT = 2
INPROJ_SPLIT = 2
D_FF = 4 * D_MODEL
N_MOD = 6
QKV_WIDTH = 3 * ATTN_WIDTH
IN_COLS = QKV_WIDTH + 3 * D_MODEL + 2 * D_MODEL
EPS = 1e-6
NEG_INF = -1e30
LOG2E = math.log2(math.e)
LN2 = math.log(2.0)
LANES = 128
SUBLANES = 8
VMEM_LIMIT = 56 * 1024 * 1024

_CB0 = QKV_WIDTH
_CC0 = _CB0 + D_MODEL
_CX0 = _CC0 + D_MODEL
_GA0 = _CX0 + D_MODEL
_GB0 = _GA0 + D_MODEL


def _resident(shape):
    zeros = (0,) * len(shape)
    return pl.BlockSpec(shape, lambda *_: zeros, pipeline_mode=pl.Buffered(1))


def _rms_modulate(xf, g, scale, shift):
    ms = jnp.mean(xf * xf, axis=-1, keepdims=True)
    return (xf * lax.rsqrt(ms + EPS) * g) * (1.0 + scale) + shift


def _ada_kernel(c_ref, w_ref, b_ref, win_ref, o_ref, win_bf_ref):
    c = c_ref[...]
    ca = c * jax.nn.sigmoid(c)
    o_ref[...] = jnp.dot(ca, w_ref[...], preferred_element_type=jnp.float32,
                         precision=lax.Precision.HIGHEST) + b_ref[...]
    win_bf_ref[...] = win_ref[...].astype(win_bf_ref.dtype)


def _ada_call(c, w_ada, b_ada, w_in):
    bsz, d = c.shape
    n = w_ada.shape[1]
    cols = (N_MOD * LANES)
    steps = n // cols
    rows = w_in.shape[0] // steps
    assert n % cols == 0 and w_in.shape[0] % steps == 0 and rows % (2 * SUBLANES) == 0
    return pl.pallas_call(
        _ada_kernel,
        grid=(steps,),
        in_specs=[pl.BlockSpec((bsz, d), lambda j: (0, 0)),
                  pl.BlockSpec((d, cols), lambda j: (0, j)),
                  pl.BlockSpec((1, cols), lambda j: (0, j)),
                  pl.BlockSpec((rows, w_in.shape[1]), lambda j: (j, 0))],
        out_specs=[pl.BlockSpec((bsz, cols), lambda j: (0, j)),
                   pl.BlockSpec((rows, w_in.shape[1]), lambda j: (j, 0))],
        out_shape=[jax.ShapeDtypeStruct((bsz, n), jnp.float32),
                   jax.ShapeDtypeStruct(w_in.shape, jnp.bfloat16)],
        compiler_params=pltpu.CompilerParams(vmem_limit_bytes=VMEM_LIMIT),
        name="ada_mod",
    )(c, w_ada, b_ada.reshape(1, n), w_in)


def _inproj_kernel(x_ref, mod_ref, g_ref, w_ref, bg_ref, cw_ref, *rest):
    n_cast = (len(rest) - 9) // 2
    cast_in, rest = rest[:n_cast], rest[n_cast:]
    qkv0_ref, qkv1_ref, qkv2_ref, cbu_ref, sa_ref, sb_ref = rest[:6]
    cast_out = rest[6:6 + n_cast]
    h_scr, carry_scr, slab_scr = rest[6 + n_cast:]
    tm = x_ref.shape[1]
    tmh = tm // INPROJ_SPLIT
    starts = [i * tmh for i in range(INPROJ_SPLIT)]
    carry_in = carry_scr.at[(pl.program_id(1) + 1) % 2]
    carry_out = carry_scr.at[pl.program_id(1) % 2]

    @pl.when(pl.program_id(1) == 0)
    def _():
        carry_in[...] = jnp.zeros_like(carry_in)

    for s0 in starts:
        rows = pl.ds(s0, tmh)
        h = _rms_modulate(x_ref[0, rows, :], g_ref[...], mod_ref[0, 1:2, :], mod_ref[0, 0:1, :])
        h_scr[0, rows, :] = h.astype(jnp.bfloat16)
        n_slab = h.shape[1] // LANES
        for c in range(n_slab):
            slab_scr[c, rows, :] = h[:, c * LANES:(c + 1) * LANES]
        for gi, (_, d) in enumerate(DILATED_GROUPS):
            if d == 1:
                continue
            for r in range(d):
                for c in range(n_slab):
                    h_scr[gi, pl.ds(s0 + r * (tmh // d), tmh // d), c * LANES:(c + 1) * LANES] = (
                        slab_scr[c, pl.ds(s0 + r, tmh // d, stride=d), :].astype(jnp.bfloat16))

    def proj(s0, c0, width, gi=0):
        return jnp.dot(h_scr[gi, pl.ds(s0, tmh), :], w_ref[:, c0:c0 + width],
                       preferred_element_type=jnp.float32)

    prev1 = carry_in[SUBLANES - 1:SUBLANES, :]
    prev2 = carry_in[SUBLANES - 2:SUBLANES - 1, :]
    for s0 in starts:
        rows = pl.ds(s0, tmh)
        p = proj(s0, _CC0, D_MODEL) * proj(s0, _CX0, D_MODEL)
        row = lax.broadcasted_iota(jnp.int32, p.shape, 0)
        p1 = jnp.where(row == 0, prev1, pltpu.roll(p, 1, 0))
        p2 = jnp.where(row == 0, prev2, jnp.where(row == 1, prev1, pltpu.roll(p, 2, 0)))
        u = cw_ref[0:1, :] * p2 + cw_ref[1:2, :] * p1 + cw_ref[2:3, :] * p
        prev1, prev2 = p[tmh - 1:tmh, :], p[tmh - 2:tmh - 1, :]
        if s0 == starts[-1]:
            carry_out[...] = p[tmh - SUBLANES:, :]
        cbu_ref[0, rows, :] = (proj(s0, _CB0, D_MODEL) * u).astype(cbu_ref.dtype)

    for s0 in starts:
        rows = pl.ds(s0, tmh)
        sa_ref[0, rows, :] = jax.nn.sigmoid(
            proj(s0, _GA0, D_MODEL) + bg_ref[:, :D_MODEL]).astype(sa_ref.dtype)
        sb_ref[0, rows, :] = jax.nn.sigmoid(
            proj(s0, _GB0, D_MODEL) + bg_ref[:, D_MODEL:]).astype(sb_ref.dtype)

    for gi, (out_ref, (_, d)) in enumerate(zip((qkv0_ref, qkv1_ref, qkv2_ref), DILATED_GROUPS)):
        for which in range(3):
            for s0 in starts:
                res = proj(s0, which * ATTN_WIDTH + gi * GROUP_WIDTH, GROUP_WIDTH, gi)
                if which == 0:
                    res = res * (HEAD_DIM ** -0.5 * LOG2E)
                o0 = which * GROUP_WIDTH
                n = tmh // d
                for r in range(d):
                    out_ref[0, r, pl.ds(s0 // d, n), o0:o0 + GROUP_WIDTH] = (
                        res[r * n:(r + 1) * n, :].astype(out_ref.dtype))

    for src, dst in zip(cast_in, cast_out):
        dst[...] = src[...].astype(dst.dtype)


def _inproj_call(x, mod, g_mix, w_in, b_gate, conv_w, tail_weights, tm):
    bsz, seq, d = x.shape
    bf = jnp.bfloat16
    n_seq = seq // tm
    n_steps = bsz * n_seq
    row_block = lambda width: pl.BlockSpec((1, tm, width), lambda b, s: (b, s, 0))
    cast_specs, cast_shapes = [], []
    for w in tail_weights:
        rows = max(2 * SUBLANES, w.shape[0] // n_steps)
        assert w.shape[0] % rows == 0 and n_steps % (w.shape[0] // rows) == 0
        reuse = n_steps // (w.shape[0] // rows)
        cast_specs.append(pl.BlockSpec(
            (rows, w.shape[1]), lambda b, s, reuse=reuse: ((b * n_seq + s) // reuse, 0)))
        cast_shapes.append(jax.ShapeDtypeStruct(w.shape, bf))
    qkv_specs, qkv_shapes = [], []
    for _, dil in DILATED_GROUPS:
        qkv_specs.append(pl.BlockSpec((1, dil, tm // dil, 3 * GROUP_WIDTH), lambda b, s: (b, 0, s, 0)))
        qkv_shapes.append(jax.ShapeDtypeStruct((bsz, dil, seq // dil, 3 * GROUP_WIDTH), bf))
    return pl.pallas_call(
        _inproj_kernel,
        grid=(bsz, seq // tm),
        in_specs=[row_block(d),
                  pl.BlockSpec((1, N_MOD, d), lambda b, s: (b, 0, 0)),
                  _resident((1, d)),
                  _resident((d, IN_COLS)),
                  _resident((1, 2 * d)),
                  _resident((CONV_K, d))] + cast_specs,
        out_specs=qkv_specs + [row_block(d), row_block(d), row_block(d)] + cast_specs,
        out_shape=qkv_shapes + [jax.ShapeDtypeStruct((bsz, seq, d), bf)] * 3 + cast_shapes,
        scratch_shapes=[pltpu.VMEM((N_GROUPS, tm, d), bf), pltpu.VMEM((2, SUBLANES, d), jnp.float32),
                        pltpu.VMEM((d // LANES, tm, LANES), jnp.float32)],
        compiler_params=pltpu.CompilerParams(
            dimension_semantics=("arbitrary", "arbitrary"), vmem_limit_bytes=VMEM_LIMIT),
        name="inproj",
    )(x, mod, g_mix, w_in, b_gate, conv_w, *tail_weights)


def _alibi_slope(head):
    return 2.0 ** (-8.0 * (head + 1) / N_ATTN_HEADS)


def _attn_kernel(*refs):
    ins, (out_ref,), scr = refs[:5 * N_GROUPS], refs[5 * N_GROUPS:5 * N_GROUPS + 1], refs[5 * N_GROUPS + 1:]
    q_refs, k_refs, v_refs, kp_refs, vp_refs = (ins[i::5] for i in range(5))
    ks, vs = scr[0:2 * N_GROUPS:2], scr[1:2 * N_GROUPS:2]
    mid_scr, far_scr, bias_scr = scr[2 * N_GROUPS:]
    head = pl.program_id(1)
    chunk = pl.program_id(2)
    first_chunk = chunk == 0

    @pl.when((pl.program_id(0) == 0) & (head == 0) & first_chunk)
    def _():
        for v_scr in vs:
            v_scr[:, :, HEAD_DIM:] = jnp.ones(v_scr.shape[:2] + (HEAD_DIM,), v_scr.dtype)
        qi = lax.broadcasted_iota(jnp.int32, (BLOCK, 2 * BLOCK), 0)
        kj = lax.broadcasted_iota(jnp.int32, (BLOCK, 2 * BLOCK), 1)
        delta = BLOCK + qi - kj
        in_window = (delta >= 0) & (delta <= BLOCK)
        for gi, (_, d) in enumerate(DILATED_GROUPS):
            dist = (delta * d).astype(jnp.float32)
            for hh in range(HEADS_PER_GROUP):
                slope = _alibi_slope(gi * HEADS_PER_GROUP + hh) * LOG2E
                bias_scr[0, hh, gi] = jnp.where(in_window, -slope * dist, NEG_INF)
                bias_scr[1, hh, gi] = jnp.where(in_window & (kj >= BLOCK), -slope * dist, NEG_INF)

    for gi in range(N_GROUPS):
        ks[gi][:, 0:BLOCK, :] = kp_refs[gi][...]
        vs[gi][:, 0:BLOCK, 0:HEAD_DIM] = vp_refs[gi][...]
        ks[gi][:, BLOCK:, :] = k_refs[gi][...]
        vs[gi][:, BLOCK:, 0:HEAD_DIM] = v_refs[gi][...]

    first_flag = first_chunk.astype(jnp.int32)

    def partials(gi, r, j):
        q = q_refs[gi][r, j * BLOCK:(j + 1) * BLOCK, :]
        kk = ks[gi][r, j * BLOCK:(j + 2) * BLOCK, :]
        vv = vs[gi][r, j * BLOCK:(j + 2) * BLOCK, :]
        bias = bias_scr[first_flag if j == 0 else 0, head, gi]
        s = lax.dot_general(q, kk, (((1,), (1,)), ((), ())),
                            preferred_element_type=jnp.float32) + bias
        m = jnp.max(s, axis=-1, keepdims=True)
        p = jnp.exp2(s - m).astype(vv.dtype)
        ov = jnp.dot(p, vv, preferred_element_type=jnp.float32)
        return ov[:, :HEAD_DIM], jnp.broadcast_to(m, (BLOCK, HEAD_DIM)), ov[:, HEAD_DIM:]

    d_mid, d_far = DILATED_GROUPS[1][1], DILATED_GROUPS[FAR][1]

    def far_rows(kind, row0):
        pieces = []
        for t in range(row0, row0 + BLOCK, SUBLANES):
            pieces.append(far_scr[kind, pl.ds((t % d_far) * FAR_PITCH + t // d_far, SUBLANES,
                                              stride=FAR_PITCH), :])
        return jnp.concatenate(pieces, axis=0)

    sub = [SPAN // d // BLOCK for _, d in DILATED_GROUPS]
    for r in range(d_far):
        for j in range(sub[FAR]):
            rows = pl.ds(r * FAR_PITCH + j * BLOCK, BLOCK)
            for kind, val in enumerate(partials(FAR, r, j)):
                far_scr[kind, rows, :] = val
    for j in range(sub[1]):
        for r in range(d_mid):
            rows = pl.ds(j * BLOCK * d_mid + r, BLOCK, stride=d_mid)
            for kind, val in enumerate(partials(1, r, j)):
                mid_scr[kind, rows, :] = val
    for j in range(sub[0]):
        rows = pl.ds(j * BLOCK, BLOCK)
        o = [partials(0, 0, j), tuple(mid_scr[kind, rows, :] for kind in range(3)),
             tuple(far_rows(kind, j * BLOCK) for kind in range(3))]
        top = jnp.maximum(jnp.maximum(o[0][1], o[1][1]), o[2][1])
        a = [jnp.exp2(part[1] - top) for part in o]
        num = a[0] * o[0][0] + a[1] * o[1][0] + a[2] * o[2][0]
        tot = a[0] * o[0][2] + a[1] * o[1][2] + a[2] * o[2][2]
        out_ref[rows, :] = (num / tot).astype(out_ref.dtype)


def _attn_call(qkvs):
    bsz = qkvs[0].shape[0]
    seq = qkvs[0].shape[1] * qkvs[0].shape[2]
    assert seq % SPAN == 0
    bf = qkvs[0].dtype
    in_specs, scratch = [], []
    for _, d in DILATED_GROUPS:
        nb = SPAN // d // BLOCK
        for which in range(3):
            in_specs.append(pl.BlockSpec(
                (None, d, SPAN // d, HEAD_DIM),
                lambda b, h, n, which=which: (b, 0, n, which * HEADS_PER_GROUP + h)))
        for which in (1, 2):
            in_specs.append(pl.BlockSpec(
                (None, d, BLOCK, HEAD_DIM),
                lambda b, h, n, which=which, nb=nb: (b, 0, jnp.maximum(n * nb - 1, 0),
                                                     which * HEADS_PER_GROUP + h)))
        scratch += [pltpu.VMEM((d, SPAN // d + BLOCK, HEAD_DIM), bf),
                    pltpu.VMEM((d, SPAN // d + BLOCK, 2 * HEAD_DIM), bf)]
    scratch += [pltpu.VMEM((3, SPAN, HEAD_DIM), jnp.float32),
                pltpu.VMEM((3, DILATED_GROUPS[FAR][1] * FAR_PITCH, HEAD_DIM), jnp.float32),
                pltpu.VMEM((2, HEADS_PER_GROUP, N_GROUPS, BLOCK, 2 * BLOCK), jnp.float32)]
    operands = [qkv for qkv in qkvs for _ in range(5)]
    return pl.pallas_call(
        _attn_kernel,
        grid=(bsz, HEADS_PER_GROUP, seq // SPAN),
        in_specs=in_specs,
        out_specs=pl.BlockSpec((None, SPAN, HEAD_DIM), lambda b, h, n: (b, n, h)),
        out_shape=jax.ShapeDtypeStruct((bsz, seq, GROUP_WIDTH), bf),
        scratch_shapes=scratch,
        compiler_params=pltpu.CompilerParams(
            dimension_semantics=("arbitrary", "arbitrary", "arbitrary"), vmem_limit_bytes=VMEM_LIMIT),
        name="dilated_attn",
    )(*operands)


def _tail_kernel(x_ref, mod_ref, oa_ref, cbu_ref, sa_ref, sb_ref, wba_ref, wbc_ref, wo_ref, gm_ref,
                 wmi_ref, wmo_ref, gf_ref, out_ref):
    bf = jnp.bfloat16
    f32 = jnp.float32
    tm = x_ref.shape[1]
    halves = tuple(pl.ds(i * (tm // TAIL_SPLIT), tm // TAIL_SPLIT) for i in range(TAIL_SPLIT))
    x1 = []
    for rows in halves:
        y_attn = jnp.dot(oa_ref[0, rows, :], wba_ref[...], preferred_element_type=f32)
        y_conv = jnp.dot(cbu_ref[0, rows, :], wbc_ref[...], preferred_element_type=f32)
        merged = sa_ref[0, rows, :].astype(f32) * y_attn + sb_ref[0, rows, :].astype(f32) * y_conv
        x1.append(x_ref[0, rows, :] + mod_ref[0, 2:3, :] * jnp.dot(
            merged.astype(bf), wo_ref[...], preferred_element_type=f32))
    act = []
    for xh in x1:
        h2 = _rms_modulate(xh, gm_ref[...], mod_ref[0, 4:5, :], mod_ref[0, 3:4, :]).astype(bf)
        a = jnp.dot(h2, wmi_ref[...], preferred_element_type=f32)
        act.append(jnp.square(jnp.maximum(a, 0.0)).astype(bf))
    for rows, xh, a in zip(halves, x1, act):
        x2 = xh + mod_ref[0, 5:6, :] * jnp.dot(a, wmo_ref[...], preferred_element_type=f32)
        ms = jnp.mean(x2 * x2, axis=-1, keepdims=True)
        out_ref[0, rows, :] = x2 * lax.rsqrt(ms + EPS) * gf_ref[...]


def _tail_call(x, mod, o_attn, cbu, sa, sb, w_ba, w_bc, w_out, g_mlp, w_mi, w_mo, g_final, tm):
    bsz, seq, d = x.shape
    row_block = lambda width: pl.BlockSpec((1, tm, width), lambda b, s: (b, s, 0))
    return pl.pallas_call(
        _tail_kernel,
        grid=(bsz, seq // tm),
        in_specs=[row_block(d),
                  pl.BlockSpec((1, N_MOD, d), lambda b, s: (b, 0, 0)),
                  row_block(GROUP_WIDTH), row_block(d), row_block(d), row_block(d),
                  _resident(w_ba.shape), _resident(w_bc.shape), _resident(w_out.shape),
                  _resident((1, d)), _resident(w_mi.shape), _resident(w_mo.shape),
                  _resident((1, d))],
        out_specs=row_block(d),
        out_shape=jax.ShapeDtypeStruct((bsz, seq, d), jnp.float32),
        compiler_params=pltpu.CompilerParams(
            dimension_semantics=("arbitrary", "arbitrary"), vmem_limit_bytes=VMEM_LIMIT),
        name="mlp_tail",
    )(x, mod, o_attn, cbu, sa, sb, w_ba, w_bc, w_out, g_mlp, w_mi, w_mo, g_final)


def kernel(x, c, w_ada, b_ada, g_norm_mix, w_in, b_gate, conv_w, w_branch_attn, w_branch_conv,
           w_out, g_norm_mlp, w_mlp_in, w_mlp_out, g_norm_final):
    bsz, seq, d = x.shape
    depth = w_ada.shape[0]
    assert depth == 1, "the final norm is fused into the (single) layer's tail"
    tm = min(512, seq)
    for window, dilation in DILATED_GROUPS:
        assert window == dilation * BLOCK and tm % (INPROJ_SPLIT * dilation * 2 * SUBLANES) == 0
    l = 0
    mod, w_in_bf = _ada_call(c, w_ada[l], b_ada[l], w_in[l])
    mod = mod.reshape(bsz, N_MOD, d)
    tail_weights = (w_branch_attn[l], w_branch_conv[l], w_out[l], w_mlp_in[l], w_mlp_out[l])
    q0, q1, q2, cbu, sa, sb, w_ba, w_bc, w_o, w_mi, w_mo = _inproj_call(
        x, mod, g_norm_mix[l].reshape(1, d), w_in_bf,
        b_gate[l].reshape(1, 2 * d), conv_w[l], tail_weights, tm)
    o_attn = _attn_call((q0, q1, q2))
    return _tail_call(x, mod, o_attn, cbu, sa, sb, w_ba, w_bc, w_o, g_norm_mlp[l].reshape(1, d),
                      w_mi, w_mo, g_norm_final.reshape(1, d), tm)
```

```python
import math

import jax
import jax.numpy as jnp
from jax import lax
from jax.experimental import pallas as pl
from jax.experimental.pallas import tpu as pltpu

D_MODEL = 1024
HEAD_DIM = 128
HEADS_PER_GROUP = 4
DILATED_GROUPS = ((128, 1), (512, 4), (2048, 16))
N_GROUPS = len(DILATED_GROUPS)
N_ATTN_HEADS = N_GROUPS * HEADS_PER_GROUP
ATTN_WIDTH = N_ATTN_HEADS * HEAD_DIM
GROUP_WIDTH = HEADS_PER_GROUP * HEAD_DIM
BLOCK = 128
SPAN = max(w for w, _ in DILATED_GROUPS)
FAR = N_GROUPS - 1
FAR_PITCH = SPAN // DILATED_GROUPS[FAR][1] + 8
CONV_K = 3
TAIL_SPLIT = 2
INPROJ_SPLIT = 2
D_FF = 4 * D_MODEL
N_MOD = 6
QKV_WIDTH = 3 * ATTN_WIDTH
IN_COLS = QKV_WIDTH + 3 * D_MODEL + 2 * D_MODEL
EPS = 1e-6
NEG_INF = -1e30
LOG2E = math.log2(math.e)
LN2 = math.log(2.0)
LANES = 128
SUBLANES = 8
VMEM_LIMIT = 56 * 1024 * 1024

_CB0 = QKV_WIDTH
_CC0 = _CB0 + D_MODEL
_CX0 = _CC0 + D_MODEL
_GA0 = _CX0 + D_MODEL
_GB0 = _GA0 + D_MODEL


def _resident(shape):
    zeros = (0,) * len(shape)
    return pl.BlockSpec(shape, lambda *_: zeros, pipeline_mode=pl.Buffered(1))


def _rms_modulate(xf, g, scale, shift):
    ms = jnp.mean(xf * xf, axis=-1, keepdims=True)
    return (xf * lax.rsqrt(ms + EPS) * g) * (1.0 + scale) + shift


def _ada_kernel(c_ref, w_ref, b_ref, win_ref, o_ref, win_bf_ref):
    c = c_ref[...]
    ca = c * jax.nn.sigmoid(c)
    o_ref[...] = jnp.dot(ca, w_ref[...], preferred_element_type=jnp.float32,
                         precision=lax.Precision.HIGHEST) + b_ref[...]
    win_bf_ref[...] = win_ref[...].astype(win_bf_ref.dtype)


def _ada_call(c, w_ada, b_ada, w_in):
    bsz, d = c.shape
    n = w_ada.shape[1]
    cols = (N_MOD * LANES)
    steps = n // cols
    rows = w_in.shape[0] // steps
    assert n % cols == 0 and w_in.shape[0] % steps == 0 and rows % (2 * SUBLANES) == 0
    return pl.pallas_call(
        _ada_kernel,
        grid=(steps,),
        in_specs=[pl.BlockSpec((bsz, d), lambda j: (0, 0)),
                  pl.BlockSpec((d, cols), lambda j: (0, j)),
                  pl.BlockSpec((1, cols), lambda j: (0, j)),
                  pl.BlockSpec((rows, w_in.shape[1]), lambda j: (j, 0))],
        out_specs=[pl.BlockSpec((bsz, cols), lambda j: (0, j)),
                   pl.BlockSpec((rows, w_in.shape[1]), lambda j: (j, 0))],
        out_shape=[jax.ShapeDtypeStruct((bsz, n), jnp.float32),
                   jax.ShapeDtypeStruct(w_in.shape, jnp.bfloat16)],
        compiler_params=pltpu.CompilerParams(vmem_limit_bytes=VMEM_LIMIT),
        name="ada_mod",
    )(c, w_ada, b_ada.reshape(1, n), w_in)


def _inproj_kernel(x_ref, mod_ref, g_ref, w_ref, bg_ref, cw_ref, *rest):
    n_cast = (len(rest) - 9) // 2
    cast_in, rest = rest[:n_cast], rest[n_cast:]
    qkv0_ref, qkv1_ref, qkv2_ref, cbu_ref, sa_ref, sb_ref = rest[:6]
    cast_out = rest[6:6 + n_cast]
    h_scr, carry_scr, slab_scr = rest[6 + n_cast:]
    tm = x_ref.shape[1]
    tmh = tm // INPROJ_SPLIT
    starts = [i * tmh for i in range(INPROJ_SPLIT)]
    carry_in = carry_scr.at[(pl.program_id(1) + 1) % 2]
    carry_out = carry_scr.at[pl.program_id(1) % 2]

    @pl.when(pl.program_id(1) == 0)
    def _():
        carry_in[...] = jnp.zeros_like(carry_in)

    for s0 in starts:
        rows = pl.ds(s0, tmh)
        h = _rms_modulate(x_ref[0, rows, :], g_ref[...], mod_ref[0, 1:2, :], mod_ref[0, 0:1, :])
        h_scr[0, rows, :] = h.astype(jnp.bfloat16)
        n_slab = h.shape[1] // LANES
        for c in range(n_slab):
            slab_scr[c, rows, :] = h[:, c * LANES:(c + 1) * LANES]
        for gi, (_, d) in enumerate(DILATED_GROUPS):
            if d == 1:
                continue
            for r in range(d):
                for c in range(n_slab):
                    h_scr[gi, pl.ds(s0 + r * (tmh // d), tmh // d), c * LANES:(c + 1) * LANES] = (
                        slab_scr[c, pl.ds(s0 + r, tmh // d, stride=d), :].astype(jnp.bfloat16))

    def proj(s0, c0, width, gi=0):
        return jnp.dot(h_scr[gi, pl.ds(s0, tmh), :], w_ref[:, c0:c0 + width],
                       preferred_element_type=jnp.float32)

    prev1 = carry_in[SUBLANES - 1:SUBLANES, :]
    prev2 = carry_in[SUBLANES - 2:SUBLANES - 1, :]
    for s0 in starts:
        rows = pl.ds(s0, tmh)
        p = proj(s0, _CC0, D_MODEL) * proj(s0, _CX0, D_MODEL)
        row = lax.broadcasted_iota(jnp.int32, p.shape, 0)
        p1 = jnp.where(row == 0, prev1, pltpu.roll(p, 1, 0))
        p2 = jnp.where(row == 0, prev2, jnp.where(row == 1, prev1, pltpu.roll(p, 2, 0)))
        u = cw_ref[0:1, :] * p2 + cw_ref[1:2, :] * p1 + cw_ref[2:3, :] * p
        prev1, prev2 = p[tmh - 1:tmh, :], p[tmh - 2:tmh - 1, :]
        if s0 == starts[-1]:
            carry_out[...] = p[tmh - SUBLANES:, :]
        cbu_ref[0, rows, :] = (proj(s0, _CB0, D_MODEL) * u).astype(cbu_ref.dtype)

    for s0 in starts:
        rows = pl.ds(s0, tmh)
        sa_ref[0, rows, :] = jax.nn.sigmoid(
            proj(s0, _GA0, D_MODEL) + bg_ref[:, :D_MODEL]).astype(sa_ref.dtype)
        sb_ref[0, rows, :] = jax.nn.sigmoid(
            proj(s0, _GB0, D_MODEL) + bg_ref[:, D_MODEL:]).astype(sb_ref.dtype)

    for gi, (out_ref, (_, d)) in enumerate(zip((qkv0_ref, qkv1_ref, qkv2_ref), DILATED_GROUPS)):
        for which in range(3):
            for s0 in starts:
                res = proj(s0, which * ATTN_WIDTH + gi * GROUP_WIDTH, GROUP_WIDTH, gi)
                if which == 0:
                    res = res * (HEAD_DIM ** -0.5 * LOG2E)
                o0 = which * GROUP_WIDTH
                n = tmh // d
                for r in range(d):
                    out_ref[0, r, pl.ds(s0 // d, n), o0:o0 + GROUP_WIDTH] = (
                        res[r * n:(r + 1) * n, :].astype(out_ref.dtype))

    for src, dst in zip(cast_in, cast_out):
        dst[...] = src[...].astype(dst.dtype)


def _inproj_call(x, mod, g_mix, w_in, b_gate, conv_w, tail_weights, tm):
    bsz, seq, d = x.shape
    bf = jnp.bfloat16
    n_seq = seq // tm
    n_steps = bsz * n_seq
    row_block = lambda width: pl.BlockSpec((1, tm, width), lambda b, s: (b, s, 0))
    cast_specs, cast_shapes = [], []
    for w in tail_weights:
        rows = max(2 * SUBLANES, w.shape[0] // n_steps)
        assert w.shape[0] % rows == 0 and n_steps % (w.shape[0] // rows) == 0
        reuse = n_steps // (w.shape[0] // rows)
        cast_specs.append(pl.BlockSpec(
            (rows, w.shape[1]), lambda b, s, reuse=reuse: ((b * n_seq + s) // reuse, 0)))
        cast_shapes.append(jax.ShapeDtypeStruct(w.shape, bf))
    qkv_specs, qkv_shapes = [], []
    for _, dil in DILATED_GROUPS:
        qkv_specs.append(pl.BlockSpec((1, dil, tm // dil, 3 * GROUP_WIDTH), lambda b, s: (b, 0, s, 0)))
        qkv_shapes.append(jax.ShapeDtypeStruct((bsz, dil, seq // dil, 3 * GROUP_WIDTH), bf))
    return pl.pallas_call(
        _inproj_kernel,
        grid=(bsz, seq // tm),
        in_specs=[row_block(d),
                  pl.BlockSpec((1, N_MOD, d), lambda b, s: (b, 0, 0)),
                  _resident((1, d)),
                  _resident((d, IN_COLS)),
                  _resident((1, 2 * d)),
                  _resident((CONV_K, d))] + cast_specs,
        out_specs=qkv_specs + [row_block(d), row_block(d), row_block(d)] + cast_specs,
        out_shape=qkv_shapes + [jax.ShapeDtypeStruct((bsz, seq, d), bf)] * 3 + cast_shapes,
        scratch_shapes=[pltpu.VMEM((N_GROUPS, tm, d), bf), pltpu.VMEM((2, SUBLANES, d), jnp.float32),
                        pltpu.VMEM((d // LANES, tm, LANES), jnp.float32)],
        compiler_params=pltpu.CompilerParams(
            dimension_semantics=("arbitrary", "arbitrary"), vmem_limit_bytes=VMEM_LIMIT),
        name="inproj",
    )(x, mod, g_mix, w_in, b_gate, conv_w, *tail_weights)


def _alibi_slope(head):
    return 2.0 ** (-8.0 * (head + 1) / N_ATTN_HEADS)


def _attn_kernel(*refs):
    ins, (out_ref,), scr = refs[:5 * N_GROUPS], refs[5 * N_GROUPS:5 * N_GROUPS + 1], refs[5 * N_GROUPS + 1:]
    q_refs, k_refs, v_refs, kp_refs, vp_refs = (ins[i::5] for i in range(5))
    ks, vs = scr[0:2 * N_GROUPS:2], scr[1:2 * N_GROUPS:2]
    mid_scr, far_scr, bias_scr = scr[2 * N_GROUPS:]
    head = pl.program_id(1)
    chunk = pl.program_id(2)
    first_chunk = chunk == 0

    @pl.when((pl.program_id(0) == 0) & (head == 0) & first_chunk)
    def _():
        for v_scr in vs:
            v_scr[:, :, HEAD_DIM:] = jnp.ones(v_scr.shape[:2] + (HEAD_DIM,), v_scr.dtype)
        qi = lax.broadcasted_iota(jnp.int32, (BLOCK, 2 * BLOCK), 0)
        kj = lax.broadcasted_iota(jnp.int32, (BLOCK, 2 * BLOCK), 1)
        delta = BLOCK + qi - kj
        in_window = (delta >= 0) & (delta <= BLOCK)
        for gi, (_, d) in enumerate(DILATED_GROUPS):
            dist = (delta * d).astype(jnp.float32)
            for hh in range(HEADS_PER_GROUP):
                slope = _alibi_slope(gi * HEADS_PER_GROUP + hh) * LOG2E
                bias_scr[0, hh, gi] = jnp.where(in_window, -slope * dist, NEG_INF)
                bias_scr[1, hh, gi] = jnp.where(in_window & (kj >= BLOCK), -slope * dist, NEG_INF)

    for gi in range(N_GROUPS):
        ks[gi][:, 0:BLOCK, :] = kp_refs[gi][...]
        vs[gi][:, 0:BLOCK, 0:HEAD_DIM] = vp_refs[gi][...]
        ks[gi][:, BLOCK:, :] = k_refs[gi][...]
        vs[gi][:, BLOCK:, 0:HEAD_DIM] = v_refs[gi][...]

    first_flag = first_chunk.astype(jnp.int32)

    def partials(gi, r, j, other_max=None):
        q = q_refs[gi][r, j * BLOCK:(j + 1) * BLOCK, :]
        kk = ks[gi][r, j * BLOCK:(j + 2) * BLOCK, :]
        vv = vs[gi][r, j * BLOCK:(j + 2) * BLOCK, :]
        bias = bias_scr[first_flag if j == 0 else 0, head, gi]
        s = lax.dot_general(q, kk, (((1,), (1,)), ((), ())),
                            preferred_element_type=jnp.float32) + bias
        m = jnp.broadcast_to(jnp.max(s, axis=-1, keepdims=True), (BLOCK, HEAD_DIM))
        if other_max is not None:
            m = jnp.maximum(m, other_max)
        p = jnp.exp2(s - jnp.concatenate([m, m], axis=1)).astype(vv.dtype)
        ov = jnp.dot(p, vv, preferred_element_type=jnp.float32)
        return ov[:, :HEAD_DIM], m, ov[:, HEAD_DIM:]

    d_mid, d_far = DILATED_GROUPS[1][1], DILATED_GROUPS[FAR][1]

    def far_rows(kind, row0):
        pieces = []
        for t in range(row0, row0 + BLOCK, SUBLANES):
            pieces.append(far_scr[kind, pl.ds((t % d_far) * FAR_PITCH + t // d_far, SUBLANES,
                                              stride=FAR_PITCH), :])
        return jnp.concatenate(pieces, axis=0)

    sub = [SPAN // d // BLOCK for _, d in DILATED_GROUPS]
    for r in range(d_far):
        for j in range(sub[FAR]):
            rows = pl.ds(r * FAR_PITCH + j * BLOCK, BLOCK)
            for kind, val in enumerate(partials(FAR, r, j)):
                far_scr[kind, rows, :] = val
    def mid_body(j, r):
        rows = pl.ds(j * BLOCK * d_mid + r, BLOCK, stride=d_mid)
        for kind, val in enumerate(partials(1, r, j)):
            mid_scr[kind, rows, :] = val

    for r in range(d_mid):
        mid_body(0, r)
    per = sub[0] // sub[1]
    for j in range(sub[0]):
        nxt, k = j // per + 1, j % per
        if nxt < sub[1]:
            for r in range(k * d_mid // per, (k + 1) * d_mid // per):
                mid_body(nxt, r)
        rows = pl.ds(j * BLOCK, BLOCK)
        mid = tuple(mid_scr[kind, rows, :] for kind in range(3))
        far = tuple(far_rows(kind, j * BLOCK) for kind in range(3))
        num, top, tot = partials(0, 0, j, other_max=jnp.maximum(mid[1], far[1]))
        for o_g, m_g, den_g in (mid, far):
            a = jnp.exp2(m_g - top)
            num += a * o_g
            tot += a * den_g
        out_ref[rows, :] = (num / tot).astype(out_ref.dtype)


def _attn_call(qkvs):
    bsz = qkvs[0].shape[0]
    seq = qkvs[0].shape[1] * qkvs[0].shape[2]
    assert seq % SPAN == 0
    bf = qkvs[0].dtype
    in_specs, scratch = [], []
    for _, d in DILATED_GROUPS:
        nb = SPAN // d // BLOCK
        for which in range(3):
            in_specs.append(pl.BlockSpec(
                (None, d, SPAN // d, HEAD_DIM),
                lambda b, h, n, which=which: (b, 0, n, which * HEADS_PER_GROUP + h)))
        for which in (1, 2):
            in_specs.append(pl.BlockSpec(
                (None, d, BLOCK, HEAD_DIM),
                lambda b, h, n, which=which, nb=nb: (b, 0, jnp.maximum(n * nb - 1, 0),
                                                     which * HEADS_PER_GROUP + h)))
        scratch += [pltpu.VMEM((d, SPAN // d + BLOCK, HEAD_DIM), bf),
                    pltpu.VMEM((d, SPAN // d + BLOCK, 2 * HEAD_DIM), bf)]
    scratch += [pltpu.VMEM((3, SPAN, HEAD_DIM), jnp.float32),
                pltpu.VMEM((3, DILATED_GROUPS[FAR][1] * FAR_PITCH, HEAD_DIM), jnp.float32),
                pltpu.VMEM((2, HEADS_PER_GROUP, N_GROUPS, BLOCK, 2 * BLOCK), jnp.float32)]
    operands = [qkv for qkv in qkvs for _ in range(5)]
    return pl.pallas_call(
        _attn_kernel,
        grid=(bsz, HEADS_PER_GROUP, seq // SPAN),
        in_specs=in_specs,
        out_specs=pl.BlockSpec((None, SPAN, HEAD_DIM), lambda b, h, n: (b, n, h)),
        out_shape=jax.ShapeDtypeStruct((bsz, seq, GROUP_WIDTH), bf),
        scratch_shapes=scratch,
        compiler_params=pltpu.CompilerParams(
            dimension_semantics=("arbitrary", "arbitrary", "arbitrary"), vmem_limit_bytes=VMEM_LIMIT),
        name="dilated_attn",
    )(*operands)


def _tail_kernel(x_ref, mod_ref, oa_ref, cbu_ref, sa_ref, sb_ref, wba_ref, wbc_ref, wo_ref, gm_ref,
                 wmi_ref, wmo_ref, gf_ref, out_ref):
    bf = jnp.bfloat16
    f32 = jnp.float32
    tm = x_ref.shape[1]
    halves = tuple(pl.ds(i * (tm // TAIL_SPLIT), tm // TAIL_SPLIT) for i in range(TAIL_SPLIT))
    x1 = []
    for rows in halves:
        y_attn = jnp.dot(oa_ref[0, rows, :], wba_ref[...], preferred_element_type=f32)
        y_conv = jnp.dot(cbu_ref[0, rows, :], wbc_ref[...], preferred_element_type=f32)
        merged = sa_ref[0, rows, :].astype(f32) * y_attn + sb_ref[0, rows, :].astype(f32) * y_conv
        x1.append(x_ref[0, rows, :] + mod_ref[0, 2:3, :] * jnp.dot(
            merged.astype(bf), wo_ref[...], preferred_element_type=f32))
    act = []
    for xh in x1:
        h2 = _rms_modulate(xh, gm_ref[...], mod_ref[0, 4:5, :], mod_ref[0, 3:4, :]).astype(bf)
        a = jnp.dot(h2, wmi_ref[...], preferred_element_type=f32)
        act.append(jnp.square(jnp.maximum(a, 0.0)).astype(bf))
    for rows, xh, a in zip(halves, x1, act):
        x2 = xh + mod_ref[0, 5:6, :] * jnp.dot(a, wmo_ref[...], preferred_element_type=f32)
        ms = jnp.mean(x2 * x2, axis=-1, keepdims=True)
        out_ref[0, rows, :] = x2 * lax.rsqrt(ms + EPS) * gf_ref[...]


def _tail_call(x, mod, o_attn, cbu, sa, sb, w_ba, w_bc, w_out, g_mlp, w_mi, w_mo, g_final, tm):
    bsz, seq, d = x.shape
    row_block = lambda width: pl.BlockSpec((1, tm, width), lambda b, s: (b, s, 0))
    return pl.pallas_call(
        _tail_kernel,
        grid=(bsz, seq // tm),
        in_specs=[row_block(d),
                  pl.BlockSpec((1, N_MOD, d), lambda b, s: (b, 0, 0)),
                  row_block(GROUP_WIDTH), row_block(d), row_block(d), row_block(d),
                  _resident(w_ba.shape), _resident(w_bc.shape), _resident(w_out.shape),
                  _resident((1, d)), _resident(w_mi.shape), _resident(w_mo.shape),
                  _resident((1, d))],
        out_specs=row_block(d),
        out_shape=jax.ShapeDtypeStruct((bsz, seq, d), jnp.float32),
        compiler_params=pltpu.CompilerParams(
            dimension_semantics=("arbitrary", "arbitrary"), vmem_limit_bytes=VMEM_LIMIT),
        name="mlp_tail",
    )(x, mod, o_attn, cbu, sa, sb, w_ba, w_bc, w_out, g_mlp, w_mi, w_mo, g_final)


def kernel(x, c, w_ada, b_ada, g_norm_mix, w_in, b_gate, conv_w, w_branch_attn, w_branch_conv,
           w_out, g_norm_mlp, w_mlp_in, w_mlp_out, g_norm_final):
    bsz, seq, d = x.shape
    depth = w_ada.shape[0]
    assert depth == 1, "the final norm is fused into the (single) layer's tail"
    tm = min(512, seq)
    for window, dilation in DILATED_GROUPS:
        assert window == dilation * BLOCK and tm % (INPROJ_SPLIT * dilation * 2 * SUBLANES) == 0
    l = 0
    mod, w_in_bf = _ada_call(c, w_ada[l], b_ada[l], w_in[l])
    mod = mod.reshape(bsz, N_MOD, d)
    tail_weights = (w_branch_attn[l], w_branch_conv[l], w_out[l], w_mlp_in[l], w_mlp_out[l])
    q0, q1, q2, cbu, sa, sb, w_ba, w_bc, w_o, w_mi, w_mo = _inproj_call(
        x, mod, g_norm_mix[l].reshape(1, d), w_in_bf,
        b_gate[l].reshape(1, 2 * d), conv_w[l], tail_weights, tm)
    o_attn = _attn_call((q0, q1, q2))
    return _tail_call(x, mod, o_attn, cbu, sa, sb, w_ba, w_bc, w_o, g_norm_mlp[l].reshape(1, d),
                      w_mi, w_mo, g_norm_final.reshape(1, d), tm)
```

```python
import math

import jax
import jax.numpy as jnp
from jax import lax
from jax.experimental import pallas as pl
from jax.experimental.pallas import tpu as pltpu

D_MODEL = 1024
HEAD_DIM = 128
HEADS_PER_GROUP = 4
DILATED_GROUPS = ((128, 1), (512, 4), (2048, 16))
N_GROUPS = len(DILATED_GROUPS)
N_ATTN_HEADS = N_GROUPS * HEADS_PER_GROUP
ATTN_WIDTH = N_ATTN_HEADS * HEAD_DIM
GROUP_WIDTH = HEADS_PER_GROUP * HEAD_DIM
BLOCK = 128
SPAN = max(w for w, _ in DILATED_GROUPS)
FAR = N_GROUPS - 1
FAR_PITCH = SPAN // DILATED_GROUPS[FAR][1] + 8
CONV_K = 3
TAIL_SPLIT = 2
INPROJ_SPLIT = 2
D_FF = 4 * D_MODEL
N_MOD = 6
QKV_WIDTH = 3 * ATTN_WIDTH
IN_COLS = QKV_WIDTH + 3 * D_MODEL + 2 * D_MODEL
EPS = 1e-6
NEG_INF = -1e30
LOG2E = math.log2(math.e)
LN2 = math.log(2.0)
LANES = 128
SUBLANES = 8
VMEM_LIMIT = 56 * 1024 * 1024

_CB0 = QKV_WIDTH
_CC0 = _CB0 + D_MODEL
_CX0 = _CC0 + D_MODEL
_GA0 = _CX0 + D_MODEL
_GB0 = _GA0 + D_MODEL


def _resident(shape):
    zeros = (0,) * len(shape)
    return pl.BlockSpec(shape, lambda *_: zeros, pipeline_mode=pl.Buffered(1))


def _rms_modulate(xf, g, scale, shift):
    ms = jnp.mean(xf * xf, axis=-1, keepdims=True)
    return (xf * lax.rsqrt(ms + EPS) * g) * (1.0 + scale) + shift


def _ada_kernel(c_ref, w_ref, b_ref, win_ref, o_ref, win_bf_ref):
    c = c_ref[...]
    ca = c * jax.nn.sigmoid(c)
    o_ref[...] = jnp.dot(ca, w_ref[...], preferred_element_type=jnp.float32,
                         precision=lax.Precision.HIGHEST) + b_ref[...]
    win_bf_ref[...] = win_ref[...].astype(win_bf_ref.dtype)


def _ada_call(c, w_ada, b_ada, w_in):
    bsz, d = c.shape
    n = w_ada.shape[1]
    cols = (N_MOD * LANES)
    steps = n // cols
    rows = w_in.shape[0] // steps
    assert n % cols == 0 and w_in.shape[0] % steps == 0 and rows % (2 * SUBLANES) == 0
    return pl.pallas_call(
        _ada_kernel,
        grid=(steps,),
        in_specs=[pl.BlockSpec((bsz, d), lambda j: (0, 0)),
                  pl.BlockSpec((d, cols), lambda j: (0, j)),
                  pl.BlockSpec((1, cols), lambda j: (0, j)),
                  pl.BlockSpec((rows, w_in.shape[1]), lambda j: (j, 0))],
        out_specs=[pl.BlockSpec((bsz, cols), lambda j: (0, j)),
                   pl.BlockSpec((rows, w_in.shape[1]), lambda j: (j, 0))],
        out_shape=[jax.ShapeDtypeStruct((bsz, n), jnp.float32),
                   jax.ShapeDtypeStruct(w_in.shape, jnp.bfloat16)],
        compiler_params=pltpu.CompilerParams(vmem_limit_bytes=VMEM_LIMIT),
        name="ada_mod",
    )(c, w_ada, b_ada.reshape(1, n), w_in)


def _inproj_kernel(x_ref, mod_ref, g_ref, w_ref, bg_ref, cw_ref, *rest):
    n_cast = (len(rest) - 9) // 2
    cast_in, rest = rest[:n_cast], rest[n_cast:]
    qkv0_ref, qkv1_ref, qkv2_ref, cbu_ref, sa_ref, sb_ref = rest[:6]
    cast_out = rest[6:6 + n_cast]
    h_scr, carry_scr, slab_scr = rest[6 + n_cast:]
    tm = x_ref.shape[1]
    tmh = tm // INPROJ_SPLIT
    starts = [i * tmh for i in range(INPROJ_SPLIT)]
    carry_in = carry_scr.at[(pl.program_id(1) + 1) % 2]
    carry_out = carry_scr.at[pl.program_id(1) % 2]

    @pl.when(pl.program_id(1) == 0)
    def _():
        carry_in[...] = jnp.zeros_like(carry_in)

    for s0 in starts:
        rows = pl.ds(s0, tmh)
        h = _rms_modulate(x_ref[0, rows, :], g_ref[...], mod_ref[0, 1:2, :], mod_ref[0, 0:1, :])
        h_scr[0, rows, :] = h.astype(jnp.bfloat16)
        n_slab = h.shape[1] // LANES
        for c in range(n_slab):
            slab_scr[c, rows, :] = h[:, c * LANES:(c + 1) * LANES]
        for gi, (_, d) in enumerate(DILATED_GROUPS):
            if d == 1:
                continue
            for r in range(d):
                for c in range(n_slab):
                    h_scr[gi, pl.ds(s0 + r * (tmh // d), tmh // d), c * LANES:(c + 1) * LANES] = (
                        slab_scr[c, pl.ds(s0 + r, tmh // d, stride=d), :].astype(jnp.bfloat16))

    def proj(s0, c0, width, gi=0):
        return jnp.dot(h_scr[gi, pl.ds(s0, tmh), :], w_ref[:, c0:c0 + width],
                       preferred_element_type=jnp.float32)

    prev1 = carry_in[SUBLANES - 1:SUBLANES, :]
    prev2 = carry_in[SUBLANES - 2:SUBLANES - 1, :]
    for s0 in starts:
        rows = pl.ds(s0, tmh)
        p = proj(s0, _CC0, D_MODEL) * proj(s0, _CX0, D_MODEL)
        row = lax.broadcasted_iota(jnp.int32, p.shape, 0)
        p1 = jnp.where(row == 0, prev1, pltpu.roll(p, 1, 0))
        p2 = jnp.where(row == 0, prev2, jnp.where(row == 1, prev1, pltpu.roll(p, 2, 0)))
        u = cw_ref[0:1, :] * p2 + cw_ref[1:2, :] * p1 + cw_ref[2:3, :] * p
        prev1, prev2 = p[tmh - 1:tmh, :], p[tmh - 2:tmh - 1, :]
        if s0 == starts[-1]:
            carry_out[...] = p[tmh - SUBLANES:, :]
        cbu_ref[0, rows, :] = (proj(s0, _CB0, D_MODEL) * u).astype(cbu_ref.dtype)

    for s0 in starts:
        rows = pl.ds(s0, tmh)
        sa_ref[0, rows, :] = jax.nn.sigmoid(
            proj(s0, _GA0, D_MODEL) + bg_ref[:, :D_MODEL]).astype(sa_ref.dtype)
        sb_ref[0, rows, :] = jax.nn.sigmoid(
            proj(s0, _GB0, D_MODEL) + bg_ref[:, D_MODEL:]).astype(sb_ref.dtype)

    for gi, (out_ref, (_, d)) in enumerate(zip((qkv0_ref, qkv1_ref, qkv2_ref), DILATED_GROUPS)):
        for which in range(3):
            for s0 in starts:
                res = proj(s0, which * ATTN_WIDTH + gi * GROUP_WIDTH, GROUP_WIDTH, gi)
                if which == 0:
                    res = res * (HEAD_DIM ** -0.5 * LOG2E)
                o0 = which * GROUP_WIDTH
                n = tmh // d
                for r in range(d):
                    out_ref[0, r, pl.ds(s0 // d, n), o0:o0 + GROUP_WIDTH] = (
                        res[r * n:(r + 1) * n, :].astype(out_ref.dtype))

    for src, dst in zip(cast_in, cast_out):
        dst[...] = src[...].astype(dst.dtype)


def _inproj_call(x, mod, g_mix, w_in, b_gate, conv_w, tail_weights, tm):
    bsz, seq, d = x.shape
    bf = jnp.bfloat16
    n_seq = seq // tm
    n_steps = bsz * n_seq
    row_block = lambda width: pl.BlockSpec((1, tm, width), lambda b, s: (b, s, 0))
    cast_specs, cast_shapes = [], []
    for w in tail_weights:
        rows = max(2 * SUBLANES, w.shape[0] // n_steps)
        assert w.shape[0] % rows == 0 and n_steps % (w.shape[0] // rows) == 0
        reuse = n_steps // (w.shape[0] // rows)
        cast_specs.append(pl.BlockSpec(
            (rows, w.shape[1]), lambda b, s, reuse=reuse: ((b * n_seq + s) // reuse, 0)))
        cast_shapes.append(jax.ShapeDtypeStruct(w.shape, bf))
    qkv_specs, qkv_shapes = [], []
    for _, dil in DILATED_GROUPS:
        qkv_specs.append(pl.BlockSpec((1, dil, tm // dil, 3 * GROUP_WIDTH), lambda b, s: (b, 0, s, 0)))
        qkv_shapes.append(jax.ShapeDtypeStruct((bsz, dil, seq // dil, 3 * GROUP_WIDTH), bf))
    return pl.pallas_call(
        _inproj_kernel,
        grid=(bsz, seq // tm),
        in_specs=[row_block(d),
                  pl.BlockSpec((1, N_MOD, d), lambda b, s: (b, 0, 0)),
                  _resident((1, d)),
                  _resident((d, IN_COLS)),
                  _resident((1, 2 * d)),
                  _resident((CONV_K, d))] + cast_specs,
        out_specs=qkv_specs + [row_block(d), row_block(d), row_block(d)] + cast_specs,
        out_shape=qkv_shapes + [jax.ShapeDtypeStruct((bsz, seq, d), bf)] * 3 + cast_shapes,
        scratch_shapes=[pltpu.VMEM((N_GROUPS, tm, d), bf), pltpu.VMEM((2, SUBLANES, d), jnp.float32),
                        pltpu.VMEM((d // LANES, tm, LANES), jnp.float32)],
        compiler_params=pltpu.CompilerParams(
            dimension_semantics=("arbitrary", "arbitrary"), vmem_limit_bytes=VMEM_LIMIT),
        name="inproj",
    )(x, mod, g_mix, w_in, b_gate, conv_w, *tail_weights)


def _alibi_slope(head):
    return 2.0 ** (-8.0 * (head + 1) / N_ATTN_HEADS)


def _attn_kernel(*refs):
    n_in = 3 * N_GROUPS + 2 * FAR
    ins, out_ref, scr = refs[:n_in], refs[n_in], refs[n_in + 1:]
    q_refs, k_refs, v_refs = ins[0:3 * N_GROUPS:3], ins[1:3 * N_GROUPS:3], ins[2:3 * N_GROUPS:3]
    kp_refs, vp_refs = ins[3 * N_GROUPS::2], ins[3 * N_GROUPS + 1::2]
    ks, vs = scr[0:2 * N_GROUPS:2], scr[1:2 * N_GROUPS:2]
    mid_scr, far_scr, bias_scr = scr[2 * N_GROUPS:]
    head = pl.program_id(1)
    chunk = pl.program_id(2)
    first_chunk = chunk == 0

    @pl.when((pl.program_id(0) == 0) & (head == 0) & first_chunk)
    def _():
        for v_scr in vs:
            v_scr[:, :, HEAD_DIM:] = jnp.ones(v_scr.shape[:2] + (HEAD_DIM,), v_scr.dtype)
        ks[FAR][:, 0:BLOCK, :] = jnp.zeros((ks[FAR].shape[0], BLOCK, HEAD_DIM), ks[FAR].dtype)
        vs[FAR][:, 0:BLOCK, 0:HEAD_DIM] = jnp.zeros((vs[FAR].shape[0], BLOCK, HEAD_DIM), vs[FAR].dtype)
        qi = lax.broadcasted_iota(jnp.int32, (BLOCK, 2 * BLOCK), 0)
        kj = lax.broadcasted_iota(jnp.int32, (BLOCK, 2 * BLOCK), 1)
        delta = BLOCK + qi - kj
        in_window = (delta >= 0) & (delta <= BLOCK)
        for gi, (_, d) in enumerate(DILATED_GROUPS):
            dist = (delta * d).astype(jnp.float32)
            for hh in range(HEADS_PER_GROUP):
                slope = _alibi_slope(gi * HEADS_PER_GROUP + hh) * LOG2E
                bias_scr[0, hh, gi] = jnp.where(in_window, -slope * dist, NEG_INF)
                bias_scr[1, hh, gi] = jnp.where(in_window & (kj >= BLOCK), -slope * dist, NEG_INF)

    for gi in range(FAR):
        ks[gi][:, 0:BLOCK, :] = kp_refs[gi][...]
        vs[gi][:, 0:BLOCK, 0:HEAD_DIM] = vp_refs[gi][...]
        ks[gi][:, BLOCK:, :] = k_refs[gi][...]
        vs[gi][:, BLOCK:, 0:HEAD_DIM] = v_refs[gi][...]
    far_new = pl.ds(pl.multiple_of((chunk + 1) * BLOCK, BLOCK), BLOCK)
    ks[FAR][:, far_new, :] = k_refs[FAR][...]
    vs[FAR][:, far_new, 0:HEAD_DIM] = v_refs[FAR][...]

    first_flag = first_chunk.astype(jnp.int32)

    def partials(gi, r, j, other_max=None):
        q = q_refs[gi][r, j * BLOCK:(j + 1) * BLOCK, :]
        if gi == FAR:
            krows = pl.ds(pl.multiple_of(chunk * BLOCK, BLOCK), 2 * BLOCK)
        else:
            krows = pl.ds(j * BLOCK, 2 * BLOCK)
        kk = ks[gi][r, krows, :]
        vv = vs[gi][r, krows, :]
        bias = bias_scr[first_flag if j == 0 else 0, head, gi]
        s = lax.dot_general(q, kk, (((1,), (1,)), ((), ())),
                            preferred_element_type=jnp.float32) + bias
        m = jnp.broadcast_to(jnp.max(s, axis=-1, keepdims=True), (BLOCK, HEAD_DIM))
        if other_max is not None:
            m = jnp.maximum(m, other_max)
        p = jnp.exp2(s - jnp.concatenate([m, m], axis=1)).astype(vv.dtype)
        ov = jnp.dot(p, vv, preferred_element_type=jnp.float32)
        return ov[:, :HEAD_DIM], m, ov[:, HEAD_DIM:]

    d_mid, d_far = DILATED_GROUPS[1][1], DILATED_GROUPS[FAR][1]

    def far_rows(kind, row0):
        pieces = []
        for t in range(row0, row0 + BLOCK, SUBLANES):
            pieces.append(far_scr[kind, pl.ds((t % d_far) * FAR_PITCH + t // d_far, SUBLANES,
                                              stride=FAR_PITCH), :])
        return jnp.concatenate(pieces, axis=0)

    sub = [SPAN // d // BLOCK for _, d in DILATED_GROUPS]
    for r in range(d_far):
        for j in range(sub[FAR]):
            rows = pl.ds(r * FAR_PITCH + j * BLOCK, BLOCK)
            for kind, val in enumerate(partials(FAR, r, j)):
                far_scr[kind, rows, :] = val
    def mid_body(j, r):
        rows = pl.ds(j * BLOCK * d_mid + r, BLOCK, stride=d_mid)
        for kind, val in enumerate(partials(1, r, j)):
            mid_scr[kind, rows, :] = val

    for r in range(d_mid):
        mid_body(0, r)
    per = sub[0] // sub[1]
    for j in range(sub[0]):
        nxt, k = j // per + 1, j % per
        if nxt < sub[1]:
            for r in range(k * d_mid // per, (k + 1) * d_mid // per):
                mid_body(nxt, r)
        rows = pl.ds(j * BLOCK, BLOCK)
        mid = tuple(mid_scr[kind, rows, :] for kind in range(3))
        far = tuple(far_rows(kind, j * BLOCK) for kind in range(3))
        num, top, tot = partials(0, 0, j, other_max=jnp.maximum(mid[1], far[1]))
        for o_g, m_g, den_g in (mid, far):
            a = jnp.exp2(m_g - top)
            num += a * o_g
            tot += a * den_g
        out_ref[rows, :] = (num / tot).astype(out_ref.dtype)


def _attn_call(qkvs):
    bsz = qkvs[0].shape[0]
    seq = qkvs[0].shape[1] * qkvs[0].shape[2]
    assert seq % SPAN == 0
    bf = qkvs[0].dtype
    n_spans = seq // SPAN
    in_specs, prev_specs, scratch = [], [], []
    for gi, (_, d) in enumerate(DILATED_GROUPS):
        nb = SPAN // d // BLOCK
        for which in range(3):
            in_specs.append(pl.BlockSpec(
                (None, d, SPAN // d, HEAD_DIM),
                lambda b, h, n, which=which: (b, 0, n, which * HEADS_PER_GROUP + h)))
        if gi == FAR:
            assert nb == 1
            k_rows = (n_spans + 1) * BLOCK
        else:
            k_rows = SPAN // d + BLOCK
            for which in (1, 2):
                prev_specs.append(pl.BlockSpec(
                    (None, d, BLOCK, HEAD_DIM),
                    lambda b, h, n, which=which, nb=nb: (b, 0, jnp.maximum(n * nb - 1, 0),
                                                         which * HEADS_PER_GROUP + h)))
        scratch += [pltpu.VMEM((d, k_rows, HEAD_DIM), bf), pltpu.VMEM((d, k_rows, 2 * HEAD_DIM), bf)]
    scratch += [pltpu.VMEM((3, SPAN, HEAD_DIM), jnp.float32),
                pltpu.VMEM((3, DILATED_GROUPS[FAR][1] * FAR_PITCH, HEAD_DIM), jnp.float32),
                pltpu.VMEM((2, HEADS_PER_GROUP, N_GROUPS, BLOCK, 2 * BLOCK), jnp.float32)]
    in_specs += prev_specs
    operands = [qkv for qkv in qkvs for _ in range(3)] + [qkv for qkv in qkvs[:FAR] for _ in range(2)]
    return pl.pallas_call(
        _attn_kernel,
        grid=(bsz, HEADS_PER_GROUP, seq // SPAN),
        in_specs=in_specs,
        out_specs=pl.BlockSpec((None, SPAN, HEAD_DIM), lambda b, h, n: (b, n, h)),
        out_shape=jax.ShapeDtypeStruct((bsz, seq, GROUP_WIDTH), bf),
        scratch_shapes=scratch,
        compiler_params=pltpu.CompilerParams(
            dimension_semantics=("arbitrary", "arbitrary", "arbitrary"), vmem_limit_bytes=VMEM_LIMIT),
        name="dilated_attn",
    )(*operands)


def _tail_kernel(x_ref, mod_ref, oa_ref, cbu_ref, sa_ref, sb_ref, wba_ref, wbc_ref, wo_ref, gm_ref,
                 wmi_ref, wmo_ref, gf_ref, out_ref):
    bf = jnp.bfloat16
    f32 = jnp.float32
    tm = x_ref.shape[1]
    halves = tuple(pl.ds(i * (tm // TAIL_SPLIT), tm // TAIL_SPLIT) for i in range(TAIL_SPLIT))
    x1 = []
    for rows in halves:
        y_attn = jnp.dot(oa_ref[0, rows, :], wba_ref[...], preferred_element_type=f32)
        y_conv = jnp.dot(cbu_ref[0, rows, :], wbc_ref[...], preferred_element_type=f32)
        merged = sa_ref[0, rows, :].astype(f32) * y_attn + sb_ref[0, rows, :].astype(f32) * y_conv
        x1.append(x_ref[0, rows, :] + mod_ref[0, 2:3, :] * jnp.dot(
            merged.astype(bf), wo_ref[...], preferred_element_type=f32))
    act = []
    for xh in x1:
        h2 = _rms_modulate(xh, gm_ref[...], mod_ref[0, 4:5, :], mod_ref[0, 3:4, :]).astype(bf)
        a = jnp.dot(h2, wmi_ref[...], preferred_element_type=f32)
        act.append(jnp.square(jnp.maximum(a, 0.0)).astype(bf))
    for rows, xh, a in zip(halves, x1, act):
        x2 = xh + mod_ref[0, 5:6, :] * jnp.dot(a, wmo_ref[...], preferred_element_type=f32)
        ms = jnp.mean(x2 * x2, axis=-1, keepdims=True)
        out_ref[0, rows, :] = x2 * lax.rsqrt(ms + EPS) * gf_ref[...]


def _tail_call(x, mod, o_attn, cbu, sa, sb, w_ba, w_bc, w_out, g_mlp, w_mi, w_mo, g_final, tm):
    bsz, seq, d = x.shape
    row_block = lambda width: pl.BlockSpec((1, tm, width), lambda b, s: (b, s, 0))
    return pl.pallas_call(
        _tail_kernel,
        grid=(bsz, seq // tm),
        in_specs=[row_block(d),
                  pl.BlockSpec((1, N_MOD, d), lambda b, s: (b, 0, 0)),
                  row_block(GROUP_WIDTH), row_block(d), row_block(d), row_block(d),
                  _resident(w_ba.shape), _resident(w_bc.shape), _resident(w_out.shape),
                  _resident((1, d)), _resident(w_mi.shape), _resident(w_mo.shape),
                  _resident((1, d))],
        out_specs=row_block(d),
        out_shape=jax.ShapeDtypeStruct((bsz, seq, d), jnp.float32),
        compiler_params=pltpu.CompilerParams(
            dimension_semantics=("arbitrary", "arbitrary"), vmem_limit_bytes=VMEM_LIMIT),
        name="mlp_tail",
    )(x, mod, o_attn, cbu, sa, sb, w_ba, w_bc, w_out, g_mlp, w_mi, w_mo, g_final)


def kernel(x, c, w_ada, b_ada, g_norm_mix, w_in, b_gate, conv_w, w_branch_attn, w_branch_conv,
           w_out, g_norm_mlp, w_mlp_in, w_mlp_out, g_norm_final):
    bsz, seq, d = x.shape
    depth = w_ada.shape[0]
    assert depth == 1, "the final norm is fused into the (single) layer's tail"
    tm = min(512, seq)
    for window, dilation in DILATED_GROUPS:
        assert window == dilation * BLOCK and tm % (INPROJ_SPLIT * dilation * 2 * SUBLANES) == 0
    l = 0
    mod, w_in_bf = _ada_call(c, w_ada[l], b_ada[l], w_in[l])
    mod = mod.reshape(bsz, N_MOD, d)
    tail_weights = (w_branch_attn[l], w_branch_conv[l], w_out[l], w_mlp_in[l], w_mlp_out[l])
    q0, q1, q2, cbu, sa, sb, w_ba, w_bc, w_o, w_mi, w_mo = _inproj_call(
        x, mod, g_norm_mix[l].reshape(1, d), w_in_bf,
        b_gate[l].reshape(1, 2 * d), conv_w[l], tail_weights, tm)
    o_attn = _attn_call((q0, q1, q2))
    return _tail_call(x, mod, o_attn, cbu, sa, sb, w_ba, w_bc, w_o, g_norm_mlp[l].reshape(1, d),
                      w_mi, w_mo, g_norm_final.reshape(1, d), tm)
```

```python
import math

import jax
import jax.numpy as jnp
from jax import lax
from jax.experimental import pallas as pl
from jax.experimental.pallas import tpu as pltpu

D_MODEL = 1024
HEAD_DIM = 128
HEADS_PER_GROUP = 4
DILATED_GROUPS = ((128, 1), (512, 4), (2048, 16))
N_GROUPS = len(DILATED_GROUPS)
N_ATTN_HEADS = N_GROUPS * HEADS_PER_GROUP
ATTN_WIDTH = N_ATTN_HEADS * HEAD_DIM
GROUP_WIDTH = HEADS_PER_GROUP * HEAD_DIM
BLOCK = 128
SPAN = max(w for w, _ in DILATED_GROUPS)
FAR = N_GROUPS - 1
CONV_K = 3
TAIL_SPLIT = 2
INPROJ_SPLIT = 2
N_MOD = 6
QKV_WIDTH = 3 * ATTN_WIDTH
IN_COLS = QKV_WIDTH + 3 * D_MODEL + 2 * D_MODEL
EPS = 1e-6
NEG_INF = -1e30
LOG2E = math.log2(math.e)
LANES = 128
SUBLANES = 8
VMEM_LIMIT = 56 * 1024 * 1024

_CB0 = QKV_WIDTH
_CC0 = _CB0 + D_MODEL
_GA0 = _CC0 + 2 * D_MODEL


def _resident(shape):
    zeros = (0,) * len(shape)
    return pl.BlockSpec(shape, lambda *_: zeros, pipeline_mode=pl.Buffered(1))


def _rms_modulate(xf, g, scale, shift):
    ms = jnp.mean(xf * xf, axis=-1, keepdims=True)
    return (xf * lax.rsqrt(ms + EPS) * g) * (1.0 + scale) + shift


def _ada_kernel(c_ref, w_ref, b_ref, win_ref, o_ref, win_bf_ref):
    c = c_ref[...]
    ca = c * jax.nn.sigmoid(c)
    o_ref[...] = jnp.dot(ca, w_ref[...], preferred_element_type=jnp.float32,
                         precision=lax.Precision.HIGHEST) + b_ref[...]
    win_bf_ref[...] = win_ref[...].astype(win_bf_ref.dtype)


def _ada_call(c, w_ada, b_ada, w_in):
    bsz, d = c.shape
    n = w_ada.shape[1]
    cols = (N_MOD * LANES)
    steps = n // cols
    rows = w_in.shape[0] // steps
    assert n % cols == 0 and w_in.shape[0] % steps == 0 and rows % (2 * SUBLANES) == 0
    return pl.pallas_call(
        _ada_kernel,
        grid=(steps,),
        in_specs=[pl.BlockSpec((bsz, d), lambda j: (0, 0)),
                  pl.BlockSpec((d, cols), lambda j: (0, j)),
                  pl.BlockSpec((1, cols), lambda j: (0, j)),
                  pl.BlockSpec((rows, w_in.shape[1]), lambda j: (j, 0))],
        out_specs=[pl.BlockSpec((bsz, cols), lambda j: (0, j)),
                   pl.BlockSpec((rows, w_in.shape[1]), lambda j: (j, 0))],
        out_shape=[jax.ShapeDtypeStruct((bsz, n), jnp.float32),
                   jax.ShapeDtypeStruct(w_in.shape, jnp.bfloat16)],
        compiler_params=pltpu.CompilerParams(vmem_limit_bytes=VMEM_LIMIT),
        name="ada_mod",
    )(c, w_ada, b_ada.reshape(1, n), w_in)


def _inproj_kernel(x_ref, mod_ref, g_ref, w_ref, bg_ref, cw_ref, *rest):
    n_cast = (len(rest) - 7) // 2
    cast_in, rest = rest[:n_cast], rest[n_cast:]
    qkv0_ref, qkv1_ref, qkv2_ref, mix_ref = rest[:4]
    cast_out = rest[4:4 + n_cast]
    h_scr, carry_scr, slab_scr = rest[4 + n_cast:]
    tm = x_ref.shape[1]
    tmh = tm // INPROJ_SPLIT
    starts = [i * tmh for i in range(INPROJ_SPLIT)]
    carry_in = carry_scr.at[(pl.program_id(1) + 1) % 2]
    carry_out = carry_scr.at[pl.program_id(1) % 2]

    @pl.when(pl.program_id(1) == 0)
    def _():
        carry_in[...] = jnp.zeros_like(carry_in)

    for s0 in starts:
        rows = pl.ds(s0, tmh)
        h = _rms_modulate(x_ref[0, rows, :], g_ref[...], mod_ref[0, 1:2, :], mod_ref[0, 0:1, :])
        h_scr[0, rows, :] = h.astype(jnp.bfloat16)
        n_slab = h.shape[1] // LANES
        for c in range(n_slab):
            slab_scr[c, rows, :] = h[:, c * LANES:(c + 1) * LANES]
        for gi, (_, d) in enumerate(DILATED_GROUPS):
            if d == 1:
                continue
            for r in range(d):
                for c in range(n_slab):
                    h_scr[gi, pl.ds(s0 + r * (tmh // d), tmh // d), c * LANES:(c + 1) * LANES] = (
                        slab_scr[c, pl.ds(s0 + r, tmh // d, stride=d), :].astype(jnp.bfloat16))

    def proj(s0, c0, width, gi=0):
        return jnp.dot(h_scr[gi, pl.ds(s0, tmh), :], w_ref[:, c0:c0 + width],
                       preferred_element_type=jnp.float32)

    prev1 = carry_in[SUBLANES - 1:SUBLANES, :]
    prev2 = carry_in[SUBLANES - 2:SUBLANES - 1, :]
    for s0 in starts:
        rows = pl.ds(s0, tmh)
        ccx = proj(s0, _CC0, 2 * D_MODEL)
        p = ccx[:, :D_MODEL] * ccx[:, D_MODEL:]
        row = lax.broadcasted_iota(jnp.int32, p.shape, 0)
        p1 = jnp.where(row == 0, prev1, pltpu.roll(p, 1, 0))
        p2 = jnp.where(row == 0, prev2, jnp.where(row == 1, prev1, pltpu.roll(p, 2, 0)))
        u = cw_ref[0:1, :] * p2 + cw_ref[1:2, :] * p1 + cw_ref[2:3, :] * p
        prev1, prev2 = p[tmh - 1:tmh, :], p[tmh - 2:tmh - 1, :]
        if s0 == starts[-1]:
            carry_out[...] = p[tmh - SUBLANES:, :]
        mix_ref[0, rows, 0:D_MODEL] = (proj(s0, _CB0, D_MODEL) * u).astype(mix_ref.dtype)

    for s0 in starts:
        rows = pl.ds(s0, tmh)
        mix_ref[0, rows, D_MODEL:] = jax.nn.sigmoid(
            proj(s0, _GA0, 2 * D_MODEL) + bg_ref[...]).astype(mix_ref.dtype)

    for gi, (out_ref, (_, d)) in enumerate(zip((qkv0_ref, qkv1_ref, qkv2_ref), DILATED_GROUPS)):
        for which in range(3):
            for s0 in starts:
                res = proj(s0, which * ATTN_WIDTH + gi * GROUP_WIDTH, GROUP_WIDTH, gi)
                if which == 0:
                    res = res * (HEAD_DIM ** -0.5 * LOG2E)
                o0 = which * GROUP_WIDTH
                n = tmh // d
                for r in range(d):
                    out_ref[0, r, pl.ds(s0 // d, n), o0:o0 + GROUP_WIDTH] = (
                        res[r * n:(r + 1) * n, :].astype(out_ref.dtype))

    for src, dst in zip(cast_in, cast_out):
        dst[...] = src[...].astype(dst.dtype)


def _inproj_call(x, mod, g_mix, w_in, b_gate, conv_w, tail_weights, tm):
    bsz, seq, d = x.shape
    bf = jnp.bfloat16
    n_seq = seq // tm
    n_steps = bsz * n_seq
    row_block = lambda width: pl.BlockSpec((1, tm, width), lambda b, s: (b, s, 0))
    cast_specs, cast_shapes = [], []
    for w in tail_weights:
        rows = max(2 * SUBLANES, w.shape[0] // n_steps)
        assert w.shape[0] % rows == 0 and n_steps % (w.shape[0] // rows) == 0
        reuse = n_steps // (w.shape[0] // rows)
        cast_specs.append(pl.BlockSpec(
            (rows, w.shape[1]), lambda b, s, reuse=reuse: ((b * n_seq + s) // reuse, 0)))
        cast_shapes.append(jax.ShapeDtypeStruct(w.shape, bf))
    qkv_specs, qkv_shapes = [], []
    for _, dil in DILATED_GROUPS:
        qkv_specs.append(pl.BlockSpec((1, dil, tm // dil, 3 * GROUP_WIDTH), lambda b, s: (b, 0, s, 0)))
        qkv_shapes.append(jax.ShapeDtypeStruct((bsz, dil, seq // dil, 3 * GROUP_WIDTH), bf))
    return pl.pallas_call(
        _inproj_kernel,
        grid=(bsz, seq // tm),
        in_specs=[row_block(d),
                  pl.BlockSpec((1, N_MOD, d), lambda b, s: (b, 0, 0)),
                  _resident((1, d)),
                  _resident((d, IN_COLS)),
                  _resident((1, 2 * d)),
                  _resident((CONV_K, d))] + cast_specs,
        out_specs=qkv_specs + [row_block(3 * d)] + cast_specs,
        out_shape=qkv_shapes + [jax.ShapeDtypeStruct((bsz, seq, 3 * d), bf)] + cast_shapes,
        scratch_shapes=[pltpu.VMEM((N_GROUPS, tm, d), bf), pltpu.VMEM((2, SUBLANES, d), jnp.float32),
                        pltpu.VMEM((d // LANES, tm, LANES), jnp.float32)],
        compiler_params=pltpu.CompilerParams(
            dimension_semantics=("arbitrary", "arbitrary"), vmem_limit_bytes=VMEM_LIMIT),
        name="inproj",
    )(x, mod, g_mix, w_in, b_gate, conv_w, *tail_weights)


def _alibi_slope(head):
    return 2.0 ** (-8.0 * (head + 1) / N_ATTN_HEADS)


def _attn_kernel(*refs):
    n_in = 3 * N_GROUPS
    ins, out_ref, scr = refs[:n_in], refs[n_in], refs[n_in + 1:]
    q_refs, k_refs, v_refs = ins[0::3], ins[1::3], ins[2::3]
    ks, vs = scr[0:2 * N_GROUPS:2], scr[1:2 * N_GROUPS:2]
    mid_scr, far_scr, bias_scr = scr[2 * N_GROUPS:]
    head = pl.program_id(1)

    @pl.when((pl.program_id(0) == 0) & (head == 0))
    def _():
        for k_scr, v_scr in zip(ks, vs):
            v_scr[:, :, HEAD_DIM:] = jnp.ones(v_scr.shape[:2] + (HEAD_DIM,), v_scr.dtype)
            k_scr[:, 0:BLOCK, :] = jnp.zeros((k_scr.shape[0], BLOCK, HEAD_DIM), k_scr.dtype)
            v_scr[:, 0:BLOCK, 0:HEAD_DIM] = jnp.zeros((v_scr.shape[0], BLOCK, HEAD_DIM), v_scr.dtype)
        qi = lax.broadcasted_iota(jnp.int32, (BLOCK, 2 * BLOCK), 0)
        kj = lax.broadcasted_iota(jnp.int32, (BLOCK, 2 * BLOCK), 1)
        delta = BLOCK + qi - kj
        in_window = (delta >= 0) & (delta <= BLOCK)
        for gi, (_, d) in enumerate(DILATED_GROUPS):
            dist = (delta * d).astype(jnp.float32)
            for hh in range(HEADS_PER_GROUP):
                slope = _alibi_slope(gi * HEADS_PER_GROUP + hh) * LOG2E
                bias_scr[0, hh, gi] = jnp.where(in_window, -slope * dist, NEG_INF)
                bias_scr[1, hh, gi] = jnp.where(in_window & (kj >= BLOCK), -slope * dist, NEG_INF)

    for gi in range(N_GROUPS):
        ks[gi][:, BLOCK:, :] = k_refs[gi][:, 0:BLOCK, :]
        vs[gi][:, BLOCK:, 0:HEAD_DIM] = v_refs[gi][...]

    def partials(gi, r, j, other_max=None):
        q = q_refs[gi][r, j * BLOCK:(j + 1) * BLOCK, :]
        kk = k_refs[gi][r, (j - 1) * BLOCK:(j + 1) * BLOCK, :] if j > 0 else ks[gi][r]
        vv = vs[gi][r, j * BLOCK:(j + 2) * BLOCK, :]
        s = lax.dot_general(q, kk, (((1,), (1,)), ((), ())),
                            preferred_element_type=jnp.float32) + bias_scr[int(j == 0), head, gi]
        m = jnp.broadcast_to(jnp.max(s, axis=-1, keepdims=True), (BLOCK, HEAD_DIM))
        if other_max is not None:
            m = jnp.maximum(m, other_max)
        p = jnp.exp2(s - jnp.concatenate([m, m], axis=1)).astype(vv.dtype)
        ov = jnp.dot(p, vv, preferred_element_type=jnp.float32)
        return ov[:, :HEAD_DIM], m, ov[:, HEAD_DIM:]

    def fold(gi, r, j, earlier):
        o_e, m_e, den_e = earlier
        num, top, tot = partials(gi, r, j, other_max=m_e)
        a = jnp.exp2(m_e - top)
        return num + a * o_e, top, tot + a * den_e

    d_mid, d_far = DILATED_GROUPS[1][1], DILATED_GROUPS[FAR][1]
    hop = d_far // d_mid
    blocks = [q_ref.shape[1] // BLOCK for q_ref in q_refs]
    for r in range(d_far):
        for j in range(blocks[FAR]):
            rows = pl.ds(hop * j * BLOCK + r // d_mid, BLOCK, stride=hop)
            for kind, val in enumerate(partials(FAR, r, j)):
                far_scr[kind, r % d_mid, rows, :] = val
    for j in range(blocks[1]):
        for r in range(d_mid):
            earlier = tuple(far_scr[kind, r, j * BLOCK:(j + 1) * BLOCK, :] for kind in range(3))
            rows = pl.ds(j * BLOCK * d_mid + r, BLOCK, stride=d_mid)
            for kind, val in enumerate(fold(1, r, j, earlier)):
                mid_scr[kind, rows, :] = val
    for j in range(blocks[0]):
        rows = pl.ds(j * BLOCK, BLOCK)
        num, _, tot = fold(0, 0, j, tuple(mid_scr[kind, rows, :] for kind in range(3)))
        out_ref[rows, :] = (num / tot).astype(out_ref.dtype)


def _attn_call(qkvs):
    bsz = qkvs[0].shape[0]
    seq = qkvs[0].shape[1] * qkvs[0].shape[2]
    assert seq % SPAN == 0
    bf = qkvs[0].dtype
    d_mid = DILATED_GROUPS[1][1]
    in_specs, scratch = [], []
    for _, d in DILATED_GROUPS:
        for which in range(3):
            in_specs.append(pl.BlockSpec(
                (None, d, seq // d, HEAD_DIM),
                lambda b, h, which=which: (b, 0, 0, which * HEADS_PER_GROUP + h)))
        scratch += [pltpu.VMEM((d, 2 * BLOCK, HEAD_DIM), bf),
                    pltpu.VMEM((d, seq // d + BLOCK, 2 * HEAD_DIM), bf)]
    scratch += [pltpu.VMEM((3, seq, HEAD_DIM), jnp.float32),
                pltpu.VMEM((3, d_mid, seq // d_mid, HEAD_DIM), jnp.float32),
                pltpu.VMEM((2, HEADS_PER_GROUP, N_GROUPS, BLOCK, 2 * BLOCK), jnp.float32)]
    operands = [qkv for qkv in qkvs for _ in range(3)]
    return pl.pallas_call(
        _attn_kernel,
        grid=(bsz, HEADS_PER_GROUP),
        in_specs=in_specs,
        out_specs=pl.BlockSpec((None, seq, HEAD_DIM), lambda b, h: (b, 0, h)),
        out_shape=jax.ShapeDtypeStruct((bsz, seq, GROUP_WIDTH), bf),
        scratch_shapes=scratch,
        compiler_params=pltpu.CompilerParams(
            dimension_semantics=("arbitrary", "arbitrary"), vmem_limit_bytes=VMEM_LIMIT),
        name="dilated_attn",
    )(*operands)


def _tail_kernel(x_ref, mod_ref, oa_ref, mix_ref, wba_ref, wbc_ref, wo_ref, gm_ref,
                 wmi_ref, wmo_ref, gf_ref, out_ref):
    bf = jnp.bfloat16
    f32 = jnp.float32
    tm = x_ref.shape[1]
    halves = tuple(pl.ds(i * (tm // TAIL_SPLIT), tm // TAIL_SPLIT) for i in range(TAIL_SPLIT))
    x1 = []
    for rows in halves:
        y_attn = jnp.dot(oa_ref[0, rows, :], wba_ref[...], preferred_element_type=f32)
        y_conv = jnp.dot(mix_ref[0, rows, 0:D_MODEL], wbc_ref[...], preferred_element_type=f32)
        merged = (mix_ref[0, rows, D_MODEL:2 * D_MODEL].astype(f32) * y_attn
                  + mix_ref[0, rows, 2 * D_MODEL:].astype(f32) * y_conv)
        x1.append(x_ref[0, rows, :] + mod_ref[0, 2:3, :] * jnp.dot(
            merged.astype(bf), wo_ref[...], preferred_element_type=f32))
    act = []
    for xh in x1:
        h2 = _rms_modulate(xh, gm_ref[...], mod_ref[0, 4:5, :], mod_ref[0, 3:4, :]).astype(bf)
        a = jnp.dot(h2, wmi_ref[...], preferred_element_type=f32)
        act.append(jnp.square(jnp.maximum(a, 0.0)).astype(bf))
    for rows, xh, a in zip(halves, x1, act):
        x2 = xh + mod_ref[0, 5:6, :] * jnp.dot(a, wmo_ref[...], preferred_element_type=f32)
        ms = jnp.mean(x2 * x2, axis=-1, keepdims=True)
        out_ref[0, rows, :] = x2 * lax.rsqrt(ms + EPS) * gf_ref[...]


def _tail_call(x, mod, o_attn, mix, w_ba, w_bc, w_out, g_mlp, w_mi, w_mo, g_final, tm):
    bsz, seq, d = x.shape
    row_block = lambda width: pl.BlockSpec((1, tm, width), lambda b, s: (b, s, 0))
    return pl.pallas_call(
        _tail_kernel,
        grid=(bsz, seq // tm),
        in_specs=[row_block(d),
                  pl.BlockSpec((1, N_MOD, d), lambda b, s: (b, 0, 0)),
                  row_block(GROUP_WIDTH), row_block(3 * d),
                  _resident(w_ba.shape), _resident(w_bc.shape), _resident(w_out.shape),
                  _resident((1, d)), _resident(w_mi.shape), _resident(w_mo.shape),
                  _resident((1, d))],
        out_specs=row_block(d),
        out_shape=jax.ShapeDtypeStruct((bsz, seq, d), jnp.float32),
        compiler_params=pltpu.CompilerParams(
            dimension_semantics=("arbitrary", "arbitrary"), vmem_limit_bytes=VMEM_LIMIT),
        name="mlp_tail",
    )(x, mod, o_attn, mix, w_ba, w_bc, w_out, g_mlp, w_mi, w_mo, g_final)


def kernel(x, c, w_ada, b_ada, g_norm_mix, w_in, b_gate, conv_w, w_branch_attn, w_branch_conv,
           w_out, g_norm_mlp, w_mlp_in, w_mlp_out, g_norm_final):
    bsz, seq, d = x.shape
    depth = w_ada.shape[0]
    assert depth == 1, "the final norm is fused into the (single) layer's tail"
    tm = min(512, seq)
    for window, dilation in DILATED_GROUPS:
        assert window == dilation * BLOCK and tm % (INPROJ_SPLIT * dilation * 2 * SUBLANES) == 0
    l = 0
    mod, w_in_bf = _ada_call(c, w_ada[l], b_ada[l], w_in[l])
    mod = mod.reshape(bsz, N_MOD, d)
    tail_weights = (w_branch_attn[l], w_branch_conv[l], w_out[l], w_mlp_in[l], w_mlp_out[l])
    q0, q1, q2, mix, w_ba, w_bc, w_o, w_mi, w_mo = _inproj_call(
        x, mod, g_norm_mix[l].reshape(1, d), w_in_bf,
        b_gate[l].reshape(1, 2 * d), conv_w[l], tail_weights, tm)
    o_attn = _attn_call((q0, q1, q2))
    return _tail_call(x, mod, o_attn, mix, w_ba, w_bc, w_o, g_norm_mlp[l].reshape(1, d),
                      w_mi, w_mo, g_norm_final.reshape(1, d), tm)
```

```python
import math

import jax
import jax.numpy as jnp
from jax import lax
from jax.experimental import pallas as pl
from jax.experimental.pallas import tpu as pltpu

D_MODEL = 1024
HEAD_DIM = 128
HEADS_PER_GROUP = 4
DILATED_GROUPS = ((128, 1), (512, 4), (2048, 16))
N_GROUPS = len(DILATED_GROUPS)
N_ATTN_HEADS = N_GROUPS * HEADS_PER_GROUP
ATTN_WIDTH = N_ATTN_HEADS * HEAD_DIM
GROUP_WIDTH = HEADS_PER_GROUP * HEAD_DIM
BLOCK = 128
SPAN = max(w for w, _ in DILATED_GROUPS)
FAR = N_GROUPS - 1
CONV_K = 3
TAIL_SPLIT = 2
INPROJ_SPLIT = 2
N_MOD = 6
QKV_WIDTH = 3 * ATTN_WIDTH
IN_COLS = QKV_WIDTH + 3 * D_MODEL + 2 * D_MODEL
EPS = 1e-6
NEG_INF = -1e30
LOG2E = math.log2(math.e)
LANES = 128
SUBLANES = 8
VMEM_LIMIT = 56 * 1024 * 1024

_CB0 = QKV_WIDTH
_CC0 = _CB0 + D_MODEL
_GA0 = _CC0 + 2 * D_MODEL


def _resident(shape):
    zeros = (0,) * len(shape)
    return pl.BlockSpec(shape, lambda *_: zeros, pipeline_mode=pl.Buffered(1))


def _mod_chunk(mod_ref, k):
    return mod_ref[pl.ds(pl.program_id(0), 1), k * D_MODEL:(k + 1) * D_MODEL]


def _rms_modulate(xf, g, scale, shift):
    ms = jnp.mean(xf * xf, axis=-1, keepdims=True)
    return (xf * lax.rsqrt(ms + EPS) * g) * (1.0 + scale) + shift


def _ada_kernel(c_ref, w_ref, b_ref, win_ref, o_ref, win_bf_ref):
    c = c_ref[...]
    ca = c * jax.nn.sigmoid(c)
    o_ref[...] = jnp.dot(ca, w_ref[...], preferred_element_type=jnp.float32,
                         precision=lax.Precision.HIGHEST) + b_ref[...]
    win_bf_ref[...] = win_ref[...].astype(win_bf_ref.dtype)


def _ada_call(c, w_ada, b_ada, w_in):
    bsz, d = c.shape
    n = w_ada.shape[1]
    cols = (N_MOD * LANES)
    steps = n // cols
    rows = w_in.shape[0] // steps
    assert n % cols == 0 and w_in.shape[0] % steps == 0 and rows % (2 * SUBLANES) == 0
    return pl.pallas_call(
        _ada_kernel,
        grid=(steps,),
        in_specs=[pl.BlockSpec((bsz, d), lambda j: (0, 0)),
                  pl.BlockSpec((d, cols), lambda j: (0, j)),
                  pl.BlockSpec((1, cols), lambda j: (0, j)),
                  pl.BlockSpec((rows, w_in.shape[1]), lambda j: (j, 0))],
        out_specs=[pl.BlockSpec((bsz, cols), lambda j: (0, j)),
                   pl.BlockSpec((rows, w_in.shape[1]), lambda j: (j, 0))],
        out_shape=[jax.ShapeDtypeStruct((bsz, n), jnp.float32),
                   jax.ShapeDtypeStruct(w_in.shape, jnp.bfloat16)],
        compiler_params=pltpu.CompilerParams(vmem_limit_bytes=VMEM_LIMIT),
        name="ada_mod",
    )(c, w_ada, b_ada.reshape(1, n), w_in)


def _inproj_kernel(x_ref, mod_ref, g_ref, w_ref, bg_ref, cw_ref, *rest):
    n_cast = (len(rest) - 7) // 2
    cast_in, rest = rest[:n_cast], rest[n_cast:]
    qkv0_ref, qkv1_ref, qkv2_ref, mix_ref = rest[:4]
    cast_out = rest[4:4 + n_cast]
    h_scr, carry_scr, slab_scr = rest[4 + n_cast:]
    tm = x_ref.shape[1]
    tmh = tm // INPROJ_SPLIT
    starts = [i * tmh for i in range(INPROJ_SPLIT)]
    carry_in = carry_scr.at[(pl.program_id(1) + 1) % 2]
    carry_out = carry_scr.at[pl.program_id(1) % 2]

    @pl.when(pl.program_id(1) == 0)
    def _():
        carry_in[...] = jnp.zeros_like(carry_in)

    for s0 in starts:
        rows = pl.ds(s0, tmh)
        h = _rms_modulate(x_ref[0, rows, :], g_ref[...], _mod_chunk(mod_ref, 1), _mod_chunk(mod_ref, 0))
        h_scr[0, rows, :] = h.astype(jnp.bfloat16)
        n_slab = h.shape[1] // LANES
        for c in range(n_slab):
            slab_scr[c, rows, :] = h[:, c * LANES:(c + 1) * LANES]
        for gi, (_, d) in enumerate(DILATED_GROUPS):
            if d == 1:
                continue
            for r in range(d):
                for c in range(n_slab):
                    h_scr[gi, pl.ds(s0 + r * (tmh // d), tmh // d), c * LANES:(c + 1) * LANES] = (
                        slab_scr[c, pl.ds(s0 + r, tmh // d, stride=d), :].astype(jnp.bfloat16))

    def proj(s0, c0, width, gi=0):
        return jnp.dot(h_scr[gi, pl.ds(s0, tmh), :], w_ref[:, c0:c0 + width],
                       preferred_element_type=jnp.float32)

    prev1 = carry_in[SUBLANES - 1:SUBLANES, :]
    prev2 = carry_in[SUBLANES - 2:SUBLANES - 1, :]
    for s0 in starts:
        rows = pl.ds(s0, tmh)
        ccx = proj(s0, _CC0, 2 * D_MODEL)
        p = ccx[:, :D_MODEL] * ccx[:, D_MODEL:]
        row = lax.broadcasted_iota(jnp.int32, p.shape, 0)
        p1 = jnp.where(row == 0, prev1, pltpu.roll(p, 1, 0))
        p2 = jnp.where(row == 0, prev2, jnp.where(row == 1, prev1, pltpu.roll(p, 2, 0)))
        u = cw_ref[0:1, :] * p2 + cw_ref[1:2, :] * p1 + cw_ref[2:3, :] * p
        prev1, prev2 = p[tmh - 1:tmh, :], p[tmh - 2:tmh - 1, :]
        if s0 == starts[-1]:
            carry_out[...] = p[tmh - SUBLANES:, :]
        mix_ref[0, rows, 0:D_MODEL] = (proj(s0, _CB0, D_MODEL) * u).astype(mix_ref.dtype)

    for s0 in starts:
        rows = pl.ds(s0, tmh)
        mix_ref[0, rows, D_MODEL:] = jax.nn.sigmoid(
            proj(s0, _GA0, 2 * D_MODEL) + bg_ref[...]).astype(mix_ref.dtype)

    for gi, (out_ref, (_, d)) in enumerate(zip((qkv0_ref, qkv1_ref, qkv2_ref), DILATED_GROUPS)):
        for which in range(3):
            for s0 in starts:
                res = proj(s0, which * ATTN_WIDTH + gi * GROUP_WIDTH, GROUP_WIDTH, gi)
                if which == 0:
                    res = res * (HEAD_DIM ** -0.5 * LOG2E)
                o0 = which * GROUP_WIDTH
                n = tmh // d
                for r in range(d):
                    out_ref[0, r, pl.ds(s0 // d, n), o0:o0 + GROUP_WIDTH] = (
                        res[r * n:(r + 1) * n, :].astype(out_ref.dtype))

    for src, dst in zip(cast_in, cast_out):
        dst[...] = src[...].astype(dst.dtype)


def _inproj_call(x, mod, g_mix, w_in, b_gate, conv_w, tail_weights, tm):
    bsz, seq, d = x.shape
    bf = jnp.bfloat16
    n_seq = seq // tm
    n_steps = bsz * n_seq
    row_block = lambda width: pl.BlockSpec((1, tm, width), lambda b, s: (b, s, 0))
    cast_specs, cast_shapes = [], []
    for w in tail_weights:
        rows = max(2 * SUBLANES, w.shape[0] // n_steps)
        assert w.shape[0] % rows == 0 and n_steps % (w.shape[0] // rows) == 0
        reuse = n_steps // (w.shape[0] // rows)
        cast_specs.append(pl.BlockSpec(
            (rows, w.shape[1]), lambda b, s, reuse=reuse: ((b * n_seq + s) // reuse, 0)))
        cast_shapes.append(jax.ShapeDtypeStruct(w.shape, bf))
    qkv_specs, qkv_shapes = [], []
    for _, dil in DILATED_GROUPS:
        qkv_specs.append(pl.BlockSpec((1, dil, tm // dil, 3 * GROUP_WIDTH), lambda b, s: (b, 0, s, 0)))
        qkv_shapes.append(jax.ShapeDtypeStruct((bsz, dil, seq // dil, 3 * GROUP_WIDTH), bf))
    return pl.pallas_call(
        _inproj_kernel,
        grid=(bsz, seq // tm),
        in_specs=[row_block(d),
                  _resident(mod.shape),
                  _resident((1, d)),
                  _resident((d, IN_COLS)),
                  _resident((1, 2 * d)),
                  _resident((CONV_K, d))] + cast_specs,
        out_specs=qkv_specs + [row_block(3 * d)] + cast_specs,
        out_shape=qkv_shapes + [jax.ShapeDtypeStruct((bsz, seq, 3 * d), bf)] + cast_shapes,
        scratch_shapes=[pltpu.VMEM((N_GROUPS, tm, d), bf), pltpu.VMEM((2, SUBLANES, d), jnp.float32),
                        pltpu.VMEM((d // LANES, tm, LANES), jnp.float32)],
        compiler_params=pltpu.CompilerParams(
            dimension_semantics=("arbitrary", "arbitrary"), vmem_limit_bytes=VMEM_LIMIT),
        name="inproj",
    )(x, mod, g_mix, w_in, b_gate, conv_w, *tail_weights)


def _alibi_slope(head):
    return 2.0 ** (-8.0 * (head + 1) / N_ATTN_HEADS)


def _attn_kernel(*refs):
    n_in = 3 * N_GROUPS
    ins, out_ref, scr = refs[:n_in], refs[n_in], refs[n_in + 1:]
    q_refs, k_refs, v_refs = ins[0::3], ins[1::3], ins[2::3]
    ks, vs = scr[0:2 * N_GROUPS:2], scr[1:2 * N_GROUPS:2]
    mid_scr, far_scr, bias_scr = scr[2 * N_GROUPS:]
    head = pl.program_id(1)

    @pl.when((pl.program_id(0) == 0) & (head == 0))
    def _():
        for k_scr, v_scr in zip(ks, vs):
            v_scr[:, :, HEAD_DIM:] = jnp.ones(v_scr.shape[:2] + (HEAD_DIM,), v_scr.dtype)
            k_scr[:, 0:BLOCK, :] = jnp.zeros((k_scr.shape[0], BLOCK, HEAD_DIM), k_scr.dtype)
            v_scr[:, 0:BLOCK, 0:HEAD_DIM] = jnp.zeros((v_scr.shape[0], BLOCK, HEAD_DIM), v_scr.dtype)
        qi = lax.broadcasted_iota(jnp.int32, (BLOCK, 2 * BLOCK), 0)
        kj = lax.broadcasted_iota(jnp.int32, (BLOCK, 2 * BLOCK), 1)
        delta = BLOCK + qi - kj
        in_window = (delta >= 0) & (delta <= BLOCK)
        for gi, (_, d) in enumerate(DILATED_GROUPS):
            dist = (delta * d).astype(jnp.float32)
            for hh in range(HEADS_PER_GROUP):
                slope = _alibi_slope(gi * HEADS_PER_GROUP + hh) * LOG2E
                bias_scr[0, hh, gi] = jnp.where(in_window, -slope * dist, NEG_INF)
                bias_scr[1, hh, gi] = jnp.where(in_window & (kj >= BLOCK), -slope * dist, NEG_INF)

    for gi in range(N_GROUPS):
        ks[gi][:, BLOCK:, :] = k_refs[gi][:, 0:BLOCK, :]
        vs[gi][:, BLOCK:, 0:HEAD_DIM] = v_refs[gi][...]

    def partials(gi, r, j, other_max=None):
        q = q_refs[gi][r, j * BLOCK:(j + 1) * BLOCK, :]
        kk = k_refs[gi][r, (j - 1) * BLOCK:(j + 1) * BLOCK, :] if j > 0 else ks[gi][r]
        vv = vs[gi][r, j * BLOCK:(j + 2) * BLOCK, :]
        s = lax.dot_general(q, kk, (((1,), (1,)), ((), ())),
                            preferred_element_type=jnp.float32) + bias_scr[int(j == 0), head, gi]
        m = jnp.broadcast_to(jnp.max(s, axis=-1, keepdims=True), (BLOCK, HEAD_DIM))
        if other_max is not None:
            m = jnp.maximum(m, other_max)
        p = jnp.exp2(s - jnp.concatenate([m, m], axis=1)).astype(vv.dtype)
        ov = jnp.dot(p, vv, preferred_element_type=jnp.float32)
        return ov[:, :HEAD_DIM], m, ov[:, HEAD_DIM:]

    def fold(gi, r, j, earlier):
        o_e, m_e, den_e = earlier
        num, top, tot = partials(gi, r, j, other_max=m_e)
        a = jnp.exp2(m_e - top)
        return num + a * o_e, top, tot + a * den_e

    d_mid, d_far = DILATED_GROUPS[1][1], DILATED_GROUPS[FAR][1]
    hop = d_far // d_mid
    blocks = [q_ref.shape[1] // BLOCK for q_ref in q_refs]
    for r in range(d_far):
        for j in range(blocks[FAR]):
            rows = pl.ds(hop * j * BLOCK + r // d_mid, BLOCK, stride=hop)
            for kind, val in enumerate(partials(FAR, r, j)):
                far_scr[kind, r % d_mid, rows, :] = val
    for j in range(blocks[1]):
        for r in range(d_mid):
            earlier = tuple(far_scr[kind, r, j * BLOCK:(j + 1) * BLOCK, :] for kind in range(3))
            rows = pl.ds(j * BLOCK * d_mid + r, BLOCK, stride=d_mid)
            for kind, val in enumerate(fold(1, r, j, earlier)):
                mid_scr[kind, rows, :] = val
    for j in range(blocks[0]):
        rows = pl.ds(j * BLOCK, BLOCK)
        num, _, tot = fold(0, 0, j, tuple(mid_scr[kind, rows, :] for kind in range(3)))
        out_ref[rows, :] = (num / tot).astype(out_ref.dtype)


def _attn_call(qkvs):
    bsz = qkvs[0].shape[0]
    seq = qkvs[0].shape[1] * qkvs[0].shape[2]
    assert seq % SPAN == 0
    bf = qkvs[0].dtype
    d_mid = DILATED_GROUPS[1][1]
    in_specs, scratch = [], []
    for _, d in DILATED_GROUPS:
        for which in range(3):
            in_specs.append(pl.BlockSpec(
                (None, d, seq // d, HEAD_DIM),
                lambda b, h, which=which: (b, 0, 0, which * HEADS_PER_GROUP + h)))
        scratch += [pltpu.VMEM((d, 2 * BLOCK, HEAD_DIM), bf),
                    pltpu.VMEM((d, seq // d + BLOCK, 2 * HEAD_DIM), bf)]
    scratch += [pltpu.VMEM((3, seq, HEAD_DIM), jnp.float32),
                pltpu.VMEM((3, d_mid, seq // d_mid, HEAD_DIM), jnp.float32),
                pltpu.VMEM((2, HEADS_PER_GROUP, N_GROUPS, BLOCK, 2 * BLOCK), jnp.float32)]
    operands = [qkv for qkv in qkvs for _ in range(3)]
    return pl.pallas_call(
        _attn_kernel,
        grid=(bsz, HEADS_PER_GROUP),
        in_specs=in_specs,
        out_specs=pl.BlockSpec((None, seq, HEAD_DIM), lambda b, h: (b, 0, h)),
        out_shape=jax.ShapeDtypeStruct((bsz, seq, GROUP_WIDTH), bf),
        scratch_shapes=scratch,
        compiler_params=pltpu.CompilerParams(
            dimension_semantics=("arbitrary", "arbitrary"), vmem_limit_bytes=VMEM_LIMIT),
        name="dilated_attn",
    )(*operands)


def _tail_kernel(x_ref, mod_ref, oa_ref, mix_ref, wba_ref, wbc_ref, wo_ref, gm_ref,
                 wmi_ref, wmo_ref, gf_ref, out_ref):
    bf = jnp.bfloat16
    f32 = jnp.float32
    tm = x_ref.shape[1]
    halves = tuple(pl.ds(i * (tm // TAIL_SPLIT), tm // TAIL_SPLIT) for i in range(TAIL_SPLIT))
    x1 = []
    for rows in halves:
        y_attn = jnp.dot(oa_ref[0, rows, :], wba_ref[...], preferred_element_type=f32)
        y_conv = jnp.dot(mix_ref[0, rows, 0:D_MODEL], wbc_ref[...], preferred_element_type=f32)
        merged = (mix_ref[0, rows, D_MODEL:2 * D_MODEL].astype(f32) * y_attn
                  + mix_ref[0, rows, 2 * D_MODEL:].astype(f32) * y_conv)
        x1.append(x_ref[0, rows, :] + _mod_chunk(mod_ref, 2) * jnp.dot(
            merged.astype(bf), wo_ref[...], preferred_element_type=f32))
    act = []
    for xh in x1:
        h2 = _rms_modulate(xh, gm_ref[...], _mod_chunk(mod_ref, 4), _mod_chunk(mod_ref, 3)).astype(bf)
        a = jnp.dot(h2, wmi_ref[...], preferred_element_type=f32)
        act.append(jnp.square(jnp.maximum(a, 0.0)).astype(bf))
    for rows, xh, a in zip(halves, x1, act):
        x2 = xh + _mod_chunk(mod_ref, 5) * jnp.dot(a, wmo_ref[...], preferred_element_type=f32)
        ms = jnp.mean(x2 * x2, axis=-1, keepdims=True)
        out_ref[0, rows, :] = x2 * lax.rsqrt(ms + EPS) * gf_ref[...]


def _tail_call(x, mod, o_attn, mix, w_ba, w_bc, w_out, g_mlp, w_mi, w_mo, g_final, tm):
    bsz, seq, d = x.shape
    row_block = lambda width: pl.BlockSpec((1, tm, width), lambda b, s: (b, s, 0))
    return pl.pallas_call(
        _tail_kernel,
        grid=(bsz, seq // tm),
        in_specs=[row_block(d),
                  _resident(mod.shape),
                  row_block(GROUP_WIDTH), row_block(3 * d),
                  _resident(w_ba.shape), _resident(w_bc.shape), _resident(w_out.shape),
                  _resident((1, d)), _resident(w_mi.shape), _resident(w_mo.shape),
                  _resident((1, d))],
        out_specs=row_block(d),
        out_shape=jax.ShapeDtypeStruct((bsz, seq, d), jnp.float32),
        compiler_params=pltpu.CompilerParams(
            dimension_semantics=("arbitrary", "arbitrary"), vmem_limit_bytes=VMEM_LIMIT),
        name="mlp_tail",
    )(x, mod, o_attn, mix, w_ba, w_bc, w_out, g_mlp, w_mi, w_mo, g_final)


def kernel(x, c, w_ada, b_ada, g_norm_mix, w_in, b_gate, conv_w, w_branch_attn, w_branch_conv,
           w_out, g_norm_mlp, w_mlp_in, w_mlp_out, g_norm_final):
    bsz, seq, d = x.shape
    depth = w_ada.shape[0]
    assert depth == 1, "the final norm is fused into the (single) layer's tail"
    tm = min(512, seq)
    for window, dilation in DILATED_GROUPS:
        assert window == dilation * BLOCK and tm % (INPROJ_SPLIT * dilation * 2 * SUBLANES) == 0
    l = 0
    mod, w_in_bf = _ada_call(c, w_ada[l], b_ada[l:l + 1], w_in[l])
    tail_weights = (w_branch_attn[l], w_branch_conv[l], w_out[l], w_mlp_in[l], w_mlp_out[l])
    q0, q1, q2, mix, w_ba, w_bc, w_o, w_mi, w_mo = _inproj_call(
        x, mod, g_norm_mix[l:l + 1], w_in_bf, b_gate[l:l + 1], conv_w[l], tail_weights, tm)
    o_attn = _attn_call((q0, q1, q2))
    return _tail_call(x, mod, o_attn, mix, w_ba, w_bc, w_o, g_norm_mlp[l:l + 1],
                      w_mi, w_mo, g_norm_final.reshape(1, d), tm)
```
